```python
import math
import jax, jax.numpy as jnp
from jax import lax
import numpy as np

D_MODEL = 1024
BATCH = 8
SEQ = 2048
DEPTH = 1
DEC_BATCH = 128
DEC_SEQ = 1
PAST_LEN = 16384
PAGE_SIZE = 128

CHUNK = 128
GMLP_WIDTH = D_MODEL
GMLP_GROUPS = 8
GMLP_GROUP_CH = GMLP_WIDTH // GMLP_GROUPS
SSM_GROUP_CH = 16
SSM_WIDTH = D_MODEL // 2
SSM_GROUPS = SSM_WIDTH // SSM_GROUP_CH
SSM_STATE = 64
IN_WIDTH = 2 * GMLP_WIDTH + SSM_WIDTH + 2 * D_MODEL
D_FF = int(math.ceil(8 * D_MODEL / 3 / 256) * 256)
EPS = 1e-6

kernel_name = "gmlp_s5_gated_hybrid_step"


def rmsnorm(x, g):
    xf = x.astype(jnp.float32)
    y = xf * lax.rsqrt(jnp.mean(xf * xf, axis=-1, keepdims=True) + EPS)
    return (y * g.astype(jnp.float32)).astype(x.dtype)


def layernorm(x, g, b):
    xf = x.astype(jnp.float32)
    mu = jnp.mean(xf, axis=-1, keepdims=True)
    var = jnp.mean(jnp.square(xf - mu), axis=-1, keepdims=True)
    y = (xf - mu) * lax.rsqrt(var + EPS)
    return (y * g.astype(jnp.float32) + b.astype(jnp.float32)).astype(x.dtype)


def chunk_spatial_gate(u, v, w_spatial, b_spatial):
    bsz, length, _ = v.shape
    n_chunks = -(-length // CHUNK)
    pad = n_chunks * CHUNK - length
    vp = jnp.pad(v, ((0, 0), (0, pad), (0, 0))).reshape(bsz, n_chunks, CHUNK, GMLP_GROUPS, GMLP_GROUP_CH)
    mask = jnp.tril(jnp.ones((CHUNK, CHUNK), dtype=bool))
    w = jnp.where(mask[None], w_spatial, jnp.zeros_like(w_spatial))
    mixed = jnp.einsum('gts,bnsgc->bntgc', w, vp) + b_spatial.T[None, None, :, :, None]
    mixed = mixed.reshape(bsz, n_chunks * CHUNK, GMLP_WIDTH)[:, :length]
    return u * mixed


def s5_scan(s, h0_re, h0_im, lam_re, lam_im, log_dt, b_re, b_im, c_re, c_im, d_skip):
    f32 = jnp.float32
    bsz, length, _ = s.shape
    u = s.astype(f32).reshape(bsz, length, SSM_GROUPS, SSM_GROUP_CH)
    lam = lax.complex(lam_re.astype(f32), lam_im.astype(f32))
    dt = jnp.exp(log_dt.astype(f32))[:, None]
    lam_bar = jnp.exp(lam * dt)
    b_mat = lax.complex(b_re.astype(f32), b_im.astype(f32))
    b_bar = ((lam_bar - 1.0) / lam)[..., None] * b_mat
    bu = jnp.einsum('gph,blgh->blgp', b_bar, u.astype(jnp.complex64))
    h0 = lax.complex(h0_re.astype(f32), h0_im.astype(f32))
    bu = bu.at[:, 0].add(lam_bar[None] * h0)
    a = jnp.broadcast_to(lam_bar, bu.shape)

    def combine(e1, e2):
        a1, b1 = e1
        a2, b2 = e2
        return a1 * a2, a2 * b1 + b2

    _, h = lax.associative_scan(combine, (a, bu), axis=1)
    c_mat = lax.complex(c_re.astype(f32), c_im.astype(f32))
    y = jnp.einsum('ghp,blgp->blgh', c_mat, h).real \
        + d_skip.astype(f32).reshape(SSM_GROUPS, SSM_GROUP_CH) * u
    h_last = h[:, -1]
    return y.reshape(bsz, length, SSM_WIDTH), jnp.real(h_last), jnp.imag(h_last)


def layer(x, h0_re, h0_im, norm_mix_g, w_in, ln_v_g, ln_v_b, w_spatial, b_spatial,
          lam_re, lam_im, log_dt, b_re, b_im, c_re, c_im, d_skip,
          w_branch_a, w_branch_b, w_out, norm_ffn_g, w_gate_ffn, w_up_ffn, w_down_ffn):
    h = rmsnorm(x, norm_mix_g)
    z = h @ w_in
    u, v, s, ga, gb = jnp.split(
        z, [GMLP_WIDTH, 2 * GMLP_WIDTH, 2 * GMLP_WIDTH + SSM_WIDTH, 2 * GMLP_WIDTH + SSM_WIDTH + D_MODEL], axis=-1)
    u = jax.nn.gelu(u)
    v = layernorm(jax.nn.gelu(v), ln_v_g, ln_v_b)
    ya = chunk_spatial_gate(u, v, w_spatial, b_spatial) @ w_branch_a
    ys, h_re, h_im = s5_scan(s, h0_re, h0_im, lam_re, lam_im, log_dt, b_re, b_im, c_re, c_im, d_skip)
    pb = ys.astype(x.dtype) @ w_branch_b
    yb = pb[..., :D_MODEL] * jax.nn.sigmoid(pb[..., D_MODEL:])
    merged = jax.nn.sigmoid(ga) * ya + jax.nn.sigmoid(gb) * yb
    x = x + merged @ w_out
    h2 = rmsnorm(x, norm_ffn_g)
    x = x + (jax.nn.silu(h2 @ w_gate_ffn) * (h2 @ w_up_ffn)) @ w_down_ffn
    return x, h_re, h_im, v


def setup_inputs(seed: int = 0) -> dict:
    key = jax.random.key(seed)
    ks = jax.random.split(key, 32)
    f32 = jnp.float32
    nrm = lambda k, shape, scale: jax.random.normal(k, shape, f32) * scale
    n_idx = jnp.arange(SSM_STATE, dtype=f32)
    lam_re = -0.5 + nrm(ks[0], (DEPTH, SSM_GROUPS, SSM_STATE), 0.01)
    lam_im = math.pi * n_idx[None, None, :] + nrm(ks[1], (DEPTH, SSM_GROUPS, SSM_STATE), 0.01)
    log_dt = jax.random.uniform(ks[2], (DEPTH, SSM_GROUPS), f32, math.log(1e-3), math.log(1e-1))
    return {
        "x_prompt": nrm(ks[3], (BATCH, SEQ, D_MODEL), 1.0),
        "x_sample": nrm(ks[4], (DEC_BATCH, DEC_SEQ, D_MODEL), 1.0),
        "state_ssm_re": nrm(ks[5], (DEPTH, DEC_BATCH, SSM_GROUPS, SSM_STATE), 1.0),
        "state_ssm_im": nrm(ks[6], (DEPTH, DEC_BATCH, SSM_GROUPS, SSM_STATE), 1.0),
        "norm_mix_g": 1.0 + nrm(ks[7], (DEPTH, D_MODEL), 0.02),
        "w_in": nrm(ks[8], (DEPTH, D_MODEL, IN_WIDTH), D_MODEL ** -0.5),
        "ln_v_g": 1.0 + nrm(ks[9], (DEPTH, GMLP_WIDTH), 0.02),
        "ln_v_b": nrm(ks[10], (DEPTH, GMLP_WIDTH), 0.02),
        "w_spatial": nrm(ks[11], (DEPTH, GMLP_GROUPS, CHUNK, CHUNK), CHUNK ** -0.5),
        "b_spatial": 1.0 + nrm(ks[12], (DEPTH, GMLP_GROUPS, CHUNK), 0.02),
        "ssm_lam_re": lam_re,
        "ssm_lam_im": lam_im,
        "ssm_log_dt": log_dt,
        "ssm_b_re": nrm(ks[13], (DEPTH, SSM_GROUPS, SSM_STATE, SSM_GROUP_CH), (2 * SSM_GROUP_CH) ** -0.5),
        "ssm_b_im": nrm(ks[14], (DEPTH, SSM_GROUPS, SSM_STATE, SSM_GROUP_CH), (2 * SSM_GROUP_CH) ** -0.5),
        "ssm_c_re": nrm(ks[15], (DEPTH, SSM_GROUPS, SSM_GROUP_CH, SSM_STATE), (2 * SSM_STATE) ** -0.5),
        "ssm_c_im": nrm(ks[16], (DEPTH, SSM_GROUPS, SSM_GROUP_CH, SSM_STATE), (2 * SSM_STATE) ** -0.5),
        "ssm_d": nrm(ks[17], (DEPTH, SSM_WIDTH), 1.0),
        "w_branch_a": nrm(ks[18], (DEPTH, GMLP_WIDTH, D_MODEL), GMLP_WIDTH ** -0.5),
        "w_branch_b": nrm(ks[19], (DEPTH, SSM_WIDTH, 2 * D_MODEL), SSM_WIDTH ** -0.5),
        "w_out": nrm(ks[20], (DEPTH, D_MODEL, D_MODEL), D_MODEL ** -0.5),
        "norm_ffn_g": 1.0 + nrm(ks[21], (DEPTH, D_MODEL), 0.02),
        "w_gate_ffn": nrm(ks[22], (DEPTH, D_MODEL, D_FF), D_MODEL ** -0.5),
        "w_up_ffn": nrm(ks[23], (DEPTH, D_MODEL, D_FF), D_MODEL ** -0.5),
        "w_down_ffn": nrm(ks[24], (DEPTH, D_FF, D_MODEL), D_FF ** -0.5),
        "norm_final_g": 1.0 + nrm(ks[25], (D_MODEL,), 0.02),
    }


def reference(x_prompt, x_sample, state_ssm_re, state_ssm_im, norm_mix_g, w_in, ln_v_g, ln_v_b,
              w_spatial, b_spatial, ssm_lam_re, ssm_lam_im, ssm_log_dt, ssm_b_re, ssm_b_im,
              ssm_c_re, ssm_c_im, ssm_d, w_branch_a, w_branch_b, w_out, norm_ffn_g,
              w_gate_ffn, w_up_ffn, w_down_ffn, norm_final_g):
    xp, xs = x_prompt, x_sample
    zeros_state = jnp.zeros((x_prompt.shape[0], SSM_GROUPS, SSM_STATE), jnp.float32)
    p_re, p_im, s_re, s_im, s_v = [], [], [], [], []
    for l in range(DEPTH):
        params = (norm_mix_g[l], w_in[l], ln_v_g[l], ln_v_b[l], w_spatial[l], b_spatial[l],
                  ssm_lam_re[l], ssm_lam_im[l], ssm_log_dt[l], ssm_b_re[l], ssm_b_im[l],
                  ssm_c_re[l], ssm_c_im[l], ssm_d[l], w_branch_a[l], w_branch_b[l], w_out[l],
                  norm_ffn_g[l], w_gate_ffn[l], w_up_ffn[l], w_down_ffn[l])
        xp, hr, hi, _ = layer(xp, zeros_state, zeros_state, *params)
        p_re.append(hr)
        p_im.append(hi)
        xs, hr, hi, v_rows = layer(xs, state_ssm_re[l], state_ssm_im[l], *params)
        s_re.append(hr)
        s_im.append(hi)
        s_v.append(v_rows)
    y_prompt = rmsnorm(xp, norm_final_g)
    y_sample = rmsnorm(xs, norm_final_g)
    new_ssm_re_prompt = jnp.stack(p_re)
    new_ssm_im_prompt = jnp.stack(p_im)
    new_ssm_re_sample = jnp.stack(s_re)
    new_ssm_im_sample = jnp.stack(s_im)
    new_chunk_v_sample = jnp.stack(s_v)
    return (y_prompt, y_sample, new_ssm_re_prompt, new_ssm_im_prompt, new_ssm_re_sample, new_ssm_im_sample, new_chunk_v_sample)
```

```python
import functools

import jax
import jax.numpy as jnp
from jax import lax
from jax.experimental import pallas as pl
from jax.experimental.pallas import tpu as pltpu

F32 = jnp.float32
BF16 = jnp.bfloat16

D_MODEL = 1024
CHUNK = 128
GMLP_GROUPS = 8
GROUP_CH = D_MODEL // GMLP_GROUPS
SSM_WIDTH = 512
SSM_GROUPS = 32
SSM_STATE = 64
SSM_GROUP_CH = 16
LANES = 128
SUBLANES = 8
SLABS = SSM_WIDTH // LANES
SLAB_GROUPS = LANES // SSM_GROUP_CH
SLAB_STATE = SLAB_GROUPS * SSM_STATE
STATE_W = SSM_GROUPS * SSM_STATE
IN_WIDTH = 2 * D_MODEL + SSM_WIDTH + 2 * D_MODEL
D_FF = 2816
EPS = 1e-6

OFF_U, OFF_V, OFF_S = 0, D_MODEL, 2 * D_MODEL
OFF_GA = 2 * D_MODEL + SSM_WIDTH
OFF_GB = OFF_GA + D_MODEL

ROW_BLOCK = 256
SCAN_STEPS = 32
VMEM_LIMIT = 56 * 1024 * 1024


def _dot(a, b):
    return jnp.dot(a, b, preferred_element_type=F32)


def _rmsnorm(x, g):
    return x * lax.rsqrt(jnp.mean(x * x, axis=-1, keepdims=True) + EPS) * g


def _layernorm(x, g, b):
    mu = jnp.mean(x, axis=-1, keepdims=True)
    xc = x - mu
    var = jnp.mean(xc * xc, axis=-1, keepdims=True)
    return xc * lax.rsqrt(var + EPS) * g + b


def _const_spec(shape):
    zeros = (0,) * len(shape)
    return pl.BlockSpec(shape, lambda *_: zeros, pipeline_mode=pl.Buffered(1))


def _ssm_prep_kernel(lre_s, lim_s, ldt_s, lre_b, lim_b, ldt_b, bre, bim,
                     are_o, aim_o, bbre_o, bbim_o):
    def lam_bar(lre, lim, ldt):
        dt = jnp.exp(ldt)
        mag = jnp.exp(lre * dt)
        return mag * jnp.cos(lim * dt), mag * jnp.sin(lim * dt)

    are, aim = lam_bar(lre_s[...], lim_s[...], ldt_s[...])
    are_o[...] = jnp.broadcast_to(are, are_o.shape)
    aim_o[...] = jnp.broadcast_to(aim, aim_o.shape)

    lre, lim = lre_b[...], lim_b[...]
    are, aim = lam_bar(lre, lim, ldt_b[...])
    nre, nim = are - 1.0, aim
    den = lre * lre + lim * lim
    cre = (nre * lre + nim * lim) / den
    cim = (nim * lre - nre * lim) / den
    bbre_o[...] = cre * bre[...] - cim * bim[...]
    bbim_o[...] = cre * bim[...] + cim * bre[...]


def _ssm_prep(lam_re, lam_im, log_dt, b_re, b_im):
    g, p, h = b_re.shape
    flat = lambda a: a.reshape(1, g * p)
    rep = lambda a: jnp.repeat(a, h, axis=1)
    ldt_gp = jnp.broadcast_to(log_dt[:, None], (g, p))
    outs = pl.pallas_call(
        _ssm_prep_kernel,
        out_shape=(jax.ShapeDtypeStruct((SUBLANES, g * p), F32),
                   jax.ShapeDtypeStruct((SUBLANES, g * p), F32),
                   jax.ShapeDtypeStruct((g, p * h), F32),
                   jax.ShapeDtypeStruct((g, p * h), F32)),
        name="ssm_prep",
    )(flat(lam_re), flat(lam_im), flat(ldt_gp), rep(lam_re), rep(lam_im), rep(ldt_gp),
      b_re.reshape(g, p * h), b_im.reshape(g, p * h))
    are, aim, bbre, bbim = outs
    return are, aim, bbre.reshape(g, p, h), bbim.reshape(g, p, h)


def _ssm_matrices(bbar_re, bbar_im, c_re, c_im):
    eye = jnp.eye(SLAB_GROUPS, dtype=F32)

    def b_side(bb):
        t = bb.reshape(SLABS, SLAB_GROUPS, SSM_STATE, SSM_GROUP_CH).transpose(0, 1, 3, 2)
        return jnp.einsum('jkhp,kl->jkhlp', t, eye).reshape(SLABS, LANES, SLAB_STATE)

    def c_side(c):
        t = c.reshape(SLABS, SLAB_GROUPS, SSM_GROUP_CH, SSM_STATE).transpose(0, 1, 3, 2)
        return jnp.einsum('jkph,kl->jkplh', t, eye).reshape(SLABS, SLAB_STATE, LANES)

    bmat = jnp.concatenate([b_side(bbar_re), b_side(bbar_im)], axis=2)
    cmat = jnp.concatenate([c_side(c_re), -c_side(c_im)], axis=1)
    return bmat.astype(BF16), cmat.astype(BF16)


def _mixer_prompt_kernel(x_ref, gmix, w_in, lng, lnb, wsp, bsp, are, aim, bmat, cmat, dsk,
                         w_a, w_b, w_out,
                         x1_ref, hre_o, him_o,
                         s_tm, ys_tm, ya_s, hn_s, bu, st_re, st_im):
    i = pl.program_id(0)
    nb = x_ref.shape[0]
    rows = nb * CHUNK
    per_block = ROW_BLOCK // CHUNK

    @pl.when(i == 0)
    def _():
        st_re[...] = jnp.zeros_like(st_re)
        st_im[...] = jnp.zeros_like(st_im)

    def phase1(blk, carry):
        r0 = pl.multiple_of(blk * ROW_BLOCK, ROW_BLOCK)
        x = jnp.concatenate([x_ref[blk * per_block + k] for k in range(per_block)], axis=0)
        hn = _rmsnorm(x, gmix[...]).astype(BF16)
        hn_s[pl.ds(r0, ROW_BLOCK), :] = hn
        u = jax.nn.gelu(_dot(hn, w_in[:, OFF_U:OFF_U + D_MODEL]))
        v = _layernorm(jax.nn.gelu(_dot(hn, w_in[:, OFF_V:OFF_V + D_MODEL])), lng[...], lnb[...])
        vb = v.astype(BF16)
        mixed = []
        for k in range(per_block):
            cols = [_dot(wsp[g], vb[k * CHUNK:(k + 1) * CHUNK, g * GROUP_CH:(g + 1) * GROUP_CH])
                    for g in range(GMLP_GROUPS)]
            mixed.append(jnp.concatenate(cols, axis=1) + bsp[...])
        gated = (u * jnp.concatenate(mixed, axis=0)).astype(BF16)
        ya_s[pl.ds(r0, ROW_BLOCK), :] = _dot(gated, w_a[...])
        s = _dot(hn, w_in[:, OFF_S:OFF_S + SSM_WIDTH])
        for k in range(per_block):
            b = blk * per_block + k
            for j in range(SLABS):
                s_tm[j, pl.ds(b, CHUNK, stride=nb), :] = s[k * CHUNK:(k + 1) * CHUNK,
                                                          j * LANES:(j + 1) * LANES]
        return carry

    lax.fori_loop(0, rows // ROW_BLOCK, phase1, 0)

    sub_rows = SCAN_STEPS * nb

    def phase2(q, carry):
        r0 = pl.multiple_of(q * sub_rows, sub_rows)
        for j in range(SLABS):
            lanes = slice(j * SLAB_STATE, (j + 1) * SLAB_STATE)
            s_blk = s_tm[j, pl.ds(r0, sub_rows), :]
            bu[...] = _dot(s_blk.astype(BF16), bmat[j])
            a_re, a_im = are[:, lanes], aim[:, lanes]

            def step(t, h):
                h_re, h_im = h
                r = pl.ds(pl.multiple_of(t * nb, nb), nb)
                n_re = a_re * h_re - a_im * h_im + bu[r, 0:SLAB_STATE]
                n_im = a_re * h_im + a_im * h_re + bu[r, SLAB_STATE:2 * SLAB_STATE]
                bu[r, 0:SLAB_STATE] = n_re
                bu[r, SLAB_STATE:2 * SLAB_STATE] = n_im
                return n_re, n_im

            h_re, h_im = lax.fori_loop(0, SCAN_STEPS, step, (st_re[:, lanes], st_im[:, lanes]),
                                       unroll=4)
            st_re[:, lanes] = h_re
            st_im[:, lanes] = h_im
            ys_tm[j, pl.ds(r0, sub_rows), :] = _dot(bu[...].astype(BF16), cmat[j]) + dsk[j] * s_blk
        return carry

    lax.fori_loop(0, rows // sub_rows, phase2, 0)

    def phase3(blk, carry):
        r0 = pl.multiple_of(blk * ROW_BLOCK, ROW_BLOCK)
        ys = jnp.concatenate(
            [jnp.concatenate([ys_tm[j, pl.ds(blk * per_block + k, CHUNK, stride=nb), :]
                              for j in range(SLABS)], axis=1)
             for k in range(per_block)], axis=0)
        pb = _dot(ys.astype(BF16), w_b[...])
        yb = pb[:, :D_MODEL] * jax.nn.sigmoid(pb[:, D_MODEL:])
        hn = hn_s[pl.ds(r0, ROW_BLOCK), :]
        ga = jax.nn.sigmoid(_dot(hn, w_in[:, OFF_GA:OFF_GA + D_MODEL]))
        gb = jax.nn.sigmoid(_dot(hn, w_in[:, OFF_GB:OFF_GB + D_MODEL]))
        merged = (ga * ya_s[pl.ds(r0, ROW_BLOCK), :] + gb * yb).astype(BF16)
        y = _dot(merged, w_out[...])
        for k in range(per_block):
            b = blk * per_block + k
            x1_ref[b] = x_ref[b] + y[k * CHUNK:(k + 1) * CHUNK]
        return carry

    lax.fori_loop(0, rows // ROW_BLOCK, phase3, 0)

    @pl.when(i == pl.num_programs(0) - 1)
    def _():
        hre_o[...] = st_re[...]
        him_o[...] = st_im[...]


def _mixer_prompt(x, gmix, w_in, lng, lnb, wsp, bsp, are, aim, bmat, cmat, dsk, w_a, w_b, w_out):
    nb, seq, d = x.shape
    rows = nb * CHUNK
    consts = (gmix, w_in, lng, lnb, wsp, bsp, are, aim, bmat, cmat, dsk, w_a, w_b, w_out)
    x_spec = pl.BlockSpec((nb, CHUNK, d), lambda i: (0, i, 0))
    st_spec = pl.BlockSpec((nb, STATE_W), lambda i: (0, 0))
    return pl.pallas_call(
        _mixer_prompt_kernel,
        grid=(seq // CHUNK,),
        in_specs=[x_spec] + [_const_spec(c.shape) for c in consts],
        out_specs=(x_spec, st_spec, st_spec),
        out_shape=(jax.ShapeDtypeStruct(x.shape, F32),
                   jax.ShapeDtypeStruct((nb, STATE_W), F32),
                   jax.ShapeDtypeStruct((nb, STATE_W), F32)),
        scratch_shapes=[
            pltpu.VMEM((SLABS, rows, LANES), F32),
            pltpu.VMEM((SLABS, rows, LANES), F32),
            pltpu.VMEM((rows, d), F32),
            pltpu.VMEM((rows, d), BF16),
            pltpu.VMEM((SCAN_STEPS * nb, 2 * SLAB_STATE), F32),
            pltpu.VMEM((nb, STATE_W), F32),
            pltpu.VMEM((nb, STATE_W), F32),
        ],
        compiler_params=pltpu.CompilerParams(dimension_semantics=("arbitrary",),
                                             vmem_limit_bytes=VMEM_LIMIT),
        name="mixer_prompt",
    )(x, *consts)


def _mixer_sample_kernel(x_ref, h0re, h0im, gmix, w_in, lng, lnb, spw, spb, are, aim, bmat, cmat,
                         dsk, w_a, w_b, w_out,
                         x1_ref, hre_o, him_o, v_o):
    x = x_ref[...]
    hn = _rmsnorm(x, gmix[...]).astype(BF16)
    u = jax.nn.gelu(_dot(hn, w_in[:, OFF_U:OFF_U + D_MODEL]))
    v = _layernorm(jax.nn.gelu(_dot(hn, w_in[:, OFF_V:OFF_V + D_MODEL])), lng[...], lnb[...])
    v_o[...] = v
    mixed = v * spw[...] + spb[...]
    ya = _dot((u * mixed).astype(BF16), w_a[...])
    s = _dot(hn, w_in[:, OFF_S:OFF_S + SSM_WIDTH])
    ys = []
    for j in range(SLABS):
        lanes = slice(j * SLAB_STATE, (j + 1) * SLAB_STATE)
        s_j = s[:, j * LANES:(j + 1) * LANES]
        bu = _dot(s_j.astype(BF16), bmat[j])
        a_re, a_im = are[0:1, lanes], aim[0:1, lanes]
        p_re, p_im = h0re[:, lanes], h0im[:, lanes]
        n_re = a_re * p_re - a_im * p_im + bu[:, 0:SLAB_STATE]
        n_im = a_re * p_im + a_im * p_re + bu[:, SLAB_STATE:]
        hre_o[:, lanes] = n_re
        him_o[:, lanes] = n_im
        h = jnp.concatenate([n_re, n_im], axis=1).astype(BF16)
        ys.append(_dot(h, cmat[j]) + dsk[j] * s_j)
    pb = _dot(jnp.concatenate(ys, axis=1).astype(BF16), w_b[...])
    yb = pb[:, :D_MODEL] * jax.nn.sigmoid(pb[:, D_MODEL:])
    ga = jax.nn.sigmoid(_dot(hn, w_in[:, OFF_GA:OFF_GA + D_MODEL]))
    gb = jax.nn.sigmoid(_dot(hn, w_in[:, OFF_GB:OFF_GB + D_MODEL]))
    merged = (ga * ya + gb * yb).astype(BF16)
    x1_ref[...] = x + _dot(merged, w_out[...])


def _mixer_sample(x, h0re, h0im, gmix, w_in, lng, lnb, spw, spb, are, aim, bmat, cmat, dsk,
                  w_a, w_b, w_out):
    n, d = x.shape
    return pl.pallas_call(
        _mixer_sample_kernel,
        out_shape=(jax.ShapeDtypeStruct((n, d), F32),
                   jax.ShapeDtypeStruct((n, STATE_W), F32),
                   jax.ShapeDtypeStruct((n, STATE_W), F32),
                   jax.ShapeDtypeStruct((n, d), F32)),
        compiler_params=pltpu.CompilerParams(vmem_limit_bytes=VMEM_LIMIT),
        name="mixer_sample",
    )(x, h0re, h0im, gmix, w_in, lng, lnb, spw, spb, are, aim, bmat, cmat, dsk, w_a, w_b, w_out)


def _ffn_kernel(x_ref, gffn, w_gate, w_up, w_down, gfin, y_ref):
    x = x_ref[...]
    h = _rmsnorm(x, gffn[...]).astype(BF16)
    act = (jax.nn.silu(_dot(h, w_gate[...])) * _dot(h, w_up[...])).astype(BF16)
    x2 = x + _dot(act, w_down[...])
    y_ref[...] = _rmsnorm(x2, gfin[...])


def _ffn(x, gffn, w_gate, w_up, w_down, gfin, block_rows):
    n, d = x.shape
    consts = (gffn, w_gate, w_up, w_down, gfin)
    row_spec = pl.BlockSpec((block_rows, d), lambda i: (i, 0))
    return pl.pallas_call(
        _ffn_kernel,
        grid=(n // block_rows,),
        in_specs=[row_spec] + [_const_spec(c.shape) for c in consts],
        out_specs=row_spec,
        out_shape=jax.ShapeDtypeStruct((n, d), F32),
        compiler_params=pltpu.CompilerParams(dimension_semantics=("arbitrary",),
                                             vmem_limit_bytes=VMEM_LIMIT),
        name="ffn",
    )(x, *consts)


def kernel(x_prompt, x_sample, state_ssm_re, state_ssm_im, norm_mix_g, w_in, ln_v_g, ln_v_b,
           w_spatial, b_spatial, ssm_lam_re, ssm_lam_im, ssm_log_dt, ssm_b_re, ssm_b_im,
           ssm_c_re, ssm_c_im, ssm_d, w_branch_a, w_branch_b, w_out, norm_ffn_g,
           w_gate_ffn, w_up_ffn, w_down_ffn, norm_final_g):
    depth = w_in.shape[0]
    assert depth == 1
    nb, seq, d = x_prompt.shape
    ns = x_sample.shape[0]
    row = lambda a: a.reshape(1, -1)

    xp = x_prompt
    xs = x_sample.reshape(ns, d)
    gfin = row(norm_final_g)
    p_re, p_im, s_re, s_im, s_v = [], [], [], [], []
    for l in range(depth):
        are, aim, bbre, bbim = _ssm_prep(ssm_lam_re[l], ssm_lam_im[l], ssm_log_dt[l],
                                         ssm_b_re[l], ssm_b_im[l])
        bmat, cmat = _ssm_matrices(bbre, bbim, ssm_c_re[l], ssm_c_im[l])
        dsk = ssm_d[l].reshape(SLABS, 1, LANES)
        tril = jnp.tril(jnp.ones((CHUNK, CHUNK), dtype=bool))
        wsp = jnp.where(tril[None], w_spatial[l], 0.0).astype(BF16)
        bsp = jnp.repeat(b_spatial[l].T, GROUP_CH, axis=1)
        spw = row(jnp.repeat(w_spatial[l][:, 0, 0], GROUP_CH))
        spb = row(jnp.repeat(b_spatial[l][:, 0], GROUP_CH))
        shared = dict(gmix=row(norm_mix_g[l]), w_in=w_in[l].astype(BF16), lng=row(ln_v_g[l]),
                      lnb=row(ln_v_b[l]))
        tail = dict(are=are, aim=aim, bmat=bmat, cmat=cmat, dsk=dsk,
                    w_a=w_branch_a[l].astype(BF16), w_b=w_branch_b[l].astype(BF16),
                    w_out=w_out[l].astype(BF16))
        ffn_w = (row(norm_ffn_g[l]), w_gate_ffn[l].astype(BF16), w_up_ffn[l].astype(BF16),
                 w_down_ffn[l].astype(BF16))

        xp, hre, him = _mixer_prompt(xp, **shared, wsp=wsp, bsp=bsp, **tail)
        p_re.append(hre.reshape(nb, SSM_GROUPS, SSM_STATE))
        p_im.append(him.reshape(nb, SSM_GROUPS, SSM_STATE))

        xs, hre, him, v_rows = _mixer_sample(
            xs, state_ssm_re[l].reshape(ns, STATE_W), state_ssm_im[l].reshape(ns, STATE_W),
            **shared, spw=spw, spb=spb, **tail)
        s_re.append(hre.reshape(ns, SSM_GROUPS, SSM_STATE))
        s_im.append(him.reshape(ns, SSM_GROUPS, SSM_STATE))
        s_v.append(v_rows.reshape(ns, 1, d))

        xp = _ffn(xp.reshape(nb * seq, d), *ffn_w, gfin, 512).reshape(nb, seq, d)
        xs = _ffn(xs, *ffn_w, gfin, ns)

    return (xp, xs.reshape(ns, 1, d), jnp.stack(p_re), jnp.stack(p_im), jnp.stack(s_re),
            jnp.stack(s_im), jnp.stack(s_v))
```

```python
import functools

import jax
import jax.numpy as jnp
from jax import lax
from jax.experimental import pallas as pl
from jax.experimental.pallas import tpu as pltpu

F32 = jnp.float32
BF16 = jnp.bfloat16

D_MODEL = 1024
CHUNK = 128
GMLP_GROUPS = 8
GROUP_CH = D_MODEL // GMLP_GROUPS
SSM_WIDTH = 512
SSM_GROUPS = 32
SSM_STATE = 64
SSM_GROUP_CH = 16
LANES = 128
SUBLANES = 8
SLABS = SSM_WIDTH // LANES
SLAB_GROUPS = LANES // SSM_GROUP_CH
SLAB_STATE = SLAB_GROUPS * SSM_STATE
STATE_W = SSM_GROUPS * SSM_STATE
IN_WIDTH = 2 * D_MODEL + SSM_WIDTH + 2 * D_MODEL
D_FF = 2816
EPS = 1e-6

OFF_U, OFF_V, OFF_S = 0, D_MODEL, 2 * D_MODEL
OFF_GA = 2 * D_MODEL + SSM_WIDTH
OFF_GB = OFF_GA + D_MODEL

ROW_BLOCK = 256
SCAN_STEPS = 32
VMEM_LIMIT = 56 * 1024 * 1024


def _dot(a, b):
    return jnp.dot(a, b, preferred_element_type=F32)


def _rmsnorm(x, g):
    return x * lax.rsqrt(jnp.mean(x * x, axis=-1, keepdims=True) + EPS) * g


def _layernorm(x, g, b):
    mu = jnp.mean(x, axis=-1, keepdims=True)
    xc = x - mu
    var = jnp.mean(xc * xc, axis=-1, keepdims=True)
    return xc * lax.rsqrt(var + EPS) * g + b


def _const_spec(shape):
    zeros = (0,) * len(shape)
    return pl.BlockSpec(shape, lambda *_: zeros, pipeline_mode=pl.Buffered(1))


def _ssm_prep_kernel(lre_s, lim_s, ldt_s, lre_b, lim_b, ldt_b, bre, bim,
                     are_o, aim_o, bbre_o, bbim_o):
    def lam_bar(lre, lim, ldt):
        dt = jnp.exp(ldt)
        mag = jnp.exp(lre * dt)
        return mag * jnp.cos(lim * dt), mag * jnp.sin(lim * dt)

    are, aim = lam_bar(lre_s[...], lim_s[...], ldt_s[...])
    are_o[...] = jnp.broadcast_to(are, are_o.shape)
    aim_o[...] = jnp.broadcast_to(aim, aim_o.shape)

    lre, lim = lre_b[...], lim_b[...]
    are, aim = lam_bar(lre, lim, ldt_b[...])
    nre, nim = are - 1.0, aim
    den = lre * lre + lim * lim
    cre = (nre * lre + nim * lim) / den
    cim = (nim * lre - nre * lim) / den
    bbre_o[...] = cre * bre[...] - cim * bim[...]
    bbim_o[...] = cre * bim[...] + cim * bre[...]


def _ssm_prep(lam_re, lam_im, log_dt, b_re, b_im):
    g, p, h = b_re.shape
    flat = lambda a: a.reshape(1, g * p)
    rep = lambda a: jnp.repeat(a, h, axis=1)
    ldt_gp = jnp.broadcast_to(log_dt[:, None], (g, p))
    outs = pl.pallas_call(
        _ssm_prep_kernel,
        out_shape=(jax.ShapeDtypeStruct((SUBLANES, g * p), F32),
                   jax.ShapeDtypeStruct((SUBLANES, g * p), F32),
                   jax.ShapeDtypeStruct((g, p * h), F32),
                   jax.ShapeDtypeStruct((g, p * h), F32)),
        name="ssm_prep",
    )(flat(lam_re), flat(lam_im), flat(ldt_gp), rep(lam_re), rep(lam_im), rep(ldt_gp),
      b_re.reshape(g, p * h), b_im.reshape(g, p * h))
    are, aim, bbre, bbim = outs
    return are, aim, bbre.reshape(g, p, h), bbim.reshape(g, p, h)


def _ssm_matrices(bbar_re, bbar_im, c_re, c_im):
    eye = jnp.eye(SLAB_GROUPS, dtype=F32)

    def b_side(bb):
        t = bb.reshape(SLABS, SLAB_GROUPS, SSM_STATE, SSM_GROUP_CH).transpose(0, 1, 3, 2)
        return jnp.einsum('jkhp,kl->jkhlp', t, eye).reshape(SLABS, LANES, SLAB_STATE)

    def c_side(c):
        t = c.reshape(SLABS, SLAB_GROUPS, SSM_GROUP_CH, SSM_STATE).transpose(0, 1, 3, 2)
        return jnp.einsum('jkph,kl->jkplh', t, eye).reshape(SLABS, SLAB_STATE, LANES)

    bmat = jnp.concatenate([b_side(bbar_re), b_side(bbar_im)], axis=2)
    cmat = jnp.concatenate([c_side(c_re), -c_side(c_im)], axis=1)
    return bmat.astype(BF16), cmat.astype(BF16)


def _mixer_prompt_kernel(x_ref, gmix, w_in, lng, lnb, wsp, bsp, are, aim, bmat, cmat, dsk,
                         w_a, w_b, w_out,
                         x1_ref, hre_o, him_o,
                         s_tm, ys_tm, ya_s, hn_s, bu, hbuf, st_re, st_im):
    i = pl.program_id(0)
    nb = x_ref.shape[0]
    rows = nb * CHUNK
    per_block = ROW_BLOCK // CHUNK

    @pl.when(i == 0)
    def _():
        st_re[...] = jnp.zeros_like(st_re)
        st_im[...] = jnp.zeros_like(st_im)

    def phase1(blk, carry):
        r0 = pl.multiple_of(blk * ROW_BLOCK, ROW_BLOCK)
        x = jnp.concatenate([x_ref[blk * per_block + k] for k in range(per_block)], axis=0)
        hn = _rmsnorm(x, gmix[...]).astype(BF16)
        hn_s[pl.ds(r0, ROW_BLOCK), :] = hn
        u = jax.nn.gelu(_dot(hn, w_in[:, OFF_U:OFF_U + D_MODEL]))
        v = _layernorm(jax.nn.gelu(_dot(hn, w_in[:, OFF_V:OFF_V + D_MODEL])), lng[...], lnb[...])
        vb = v.astype(BF16)
        mixed = []
        for k in range(per_block):
            cols = [_dot(wsp[g], vb[k * CHUNK:(k + 1) * CHUNK, g * GROUP_CH:(g + 1) * GROUP_CH])
                    for g in range(GMLP_GROUPS)]
            mixed.append(jnp.concatenate(cols, axis=1) + bsp[...])
        gated = (u * jnp.concatenate(mixed, axis=0)).astype(BF16)
        ya_s[pl.ds(r0, ROW_BLOCK), :] = _dot(gated, w_a[...])
        s = _dot(hn, w_in[:, OFF_S:OFF_S + SSM_WIDTH])
        for k in range(per_block):
            b = blk * per_block + k
            for j in range(SLABS):
                s_tm[j, pl.ds(b, CHUNK, stride=nb), :] = s[k * CHUNK:(k + 1) * CHUNK,
                                                          j * LANES:(j + 1) * LANES]
        return carry

    lax.fori_loop(0, rows // ROW_BLOCK, phase1, 0, unroll=2)

    sub_rows = SCAN_STEPS * nb

    def phase2(q, carry):
        r0 = pl.multiple_of(q * sub_rows, sub_rows)
        for j in range(SLABS):
            lanes = slice(j * SLAB_STATE, (j + 1) * SLAB_STATE)
            s_blk = s_tm[j, pl.ds(r0, sub_rows), :]
            bu[j] = _dot(s_blk.astype(BF16), bmat[j])
            a_re, a_im = are[:, lanes], aim[:, lanes]
            h_re, h_im = st_re[:, lanes], st_im[:, lanes]
            for t in range(0, SCAN_STEPS, 2):
                pair = []
                for r in (slice(t * nb, (t + 1) * nb), slice((t + 1) * nb, (t + 2) * nb)):
                    n_re = a_re * h_re - a_im * h_im + bu[j, r, 0:SLAB_STATE]
                    n_im = a_re * h_im + a_im * h_re + bu[j, r, SLAB_STATE:2 * SLAB_STATE]
                    h_re, h_im = n_re, n_im
                    pair.append(jnp.concatenate([n_re, n_im], axis=1))
                hbuf[j, t * nb:(t + 2) * nb, :] = jnp.concatenate(pair, axis=0).astype(BF16)
            st_re[:, lanes] = h_re
            st_im[:, lanes] = h_im
            ys_tm[j, pl.ds(r0, sub_rows), :] = _dot(hbuf[j], cmat[j]) + dsk[j] * s_blk
        return carry

    lax.fori_loop(0, rows // sub_rows, phase2, 0)

    def phase3(blk, carry):
        r0 = pl.multiple_of(blk * ROW_BLOCK, ROW_BLOCK)
        ys = jnp.concatenate(
            [jnp.concatenate([ys_tm[j, pl.ds(blk * per_block + k, CHUNK, stride=nb), :]
                              for j in range(SLABS)], axis=1)
             for k in range(per_block)], axis=0)
        pb = _dot(ys.astype(BF16), w_b[...])
        yb = pb[:, :D_MODEL] * jax.nn.sigmoid(pb[:, D_MODEL:])
        hn = hn_s[pl.ds(r0, ROW_BLOCK), :]
        ga = jax.nn.sigmoid(_dot(hn, w_in[:, OFF_GA:OFF_GA + D_MODEL]))
        gb = jax.nn.sigmoid(_dot(hn, w_in[:, OFF_GB:OFF_GB + D_MODEL]))
        merged = (ga * ya_s[pl.ds(r0, ROW_BLOCK), :] + gb * yb).astype(BF16)
        y = _dot(merged, w_out[...])
        for k in range(per_block):
            b = blk * per_block + k
            x1_ref[b] = x_ref[b] + y[k * CHUNK:(k + 1) * CHUNK]
        return carry

    lax.fori_loop(0, rows // ROW_BLOCK, phase3, 0, unroll=2)

    @pl.when(i == pl.num_programs(0) - 1)
    def _():
        hre_o[...] = st_re[...]
        him_o[...] = st_im[...]


def _mixer_prompt(x, gmix, w_in, lng, lnb, wsp, bsp, are, aim, bmat, cmat, dsk, w_a, w_b, w_out):
    nb, seq, d = x.shape
    rows = nb * CHUNK
    consts = (gmix, w_in, lng, lnb, wsp, bsp, are, aim, bmat, cmat, dsk, w_a, w_b, w_out)
    x_spec = pl.BlockSpec((nb, CHUNK, d), lambda i: (0, i, 0))
    st_spec = pl.BlockSpec((nb, STATE_W), lambda i: (0, 0))
    return pl.pallas_call(
        _mixer_prompt_kernel,
        grid=(seq // CHUNK,),
        in_specs=[x_spec] + [_const_spec(c.shape) for c in consts],
        out_specs=(x_spec, st_spec, st_spec),
        out_shape=(jax.ShapeDtypeStruct(x.shape, F32),
                   jax.ShapeDtypeStruct((nb, STATE_W), F32),
                   jax.ShapeDtypeStruct((nb, STATE_W), F32)),
        scratch_shapes=[
            pltpu.VMEM((SLABS, rows, LANES), F32),
            pltpu.VMEM((SLABS, rows, LANES), F32),
            pltpu.VMEM((rows, d), F32),
            pltpu.VMEM((rows, d), BF16),
            pltpu.VMEM((SLABS, SCAN_STEPS * nb, 2 * SLAB_STATE), F32),
            pltpu.VMEM((SLABS, SCAN_STEPS * nb, 2 * SLAB_STATE), BF16),
            pltpu.VMEM((nb, STATE_W), F32),
            pltpu.VMEM((nb, STATE_W), F32),
        ],
        compiler_params=pltpu.CompilerParams(dimension_semantics=("arbitrary",),
                                             vmem_limit_bytes=VMEM_LIMIT),
        name="mixer_prompt",
    )(x, *consts)


def _mixer_sample_kernel(x_ref, h0re, h0im, gmix, w_in, lng, lnb, spw, spb, are, aim, bmat, cmat,
                         dsk, w_a, w_b, w_out,
                         x1_ref, hre_o, him_o, v_o):
    x = x_ref[...]
    hn = _rmsnorm(x, gmix[...]).astype(BF16)
    u = jax.nn.gelu(_dot(hn, w_in[:, OFF_U:OFF_U + D_MODEL]))
    v = _layernorm(jax.nn.gelu(_dot(hn, w_in[:, OFF_V:OFF_V + D_MODEL])), lng[...], lnb[...])
    v_o[...] = v
    mixed = v * spw[...] + spb[...]
    ya = _dot((u * mixed).astype(BF16), w_a[...])
    s = _dot(hn, w_in[:, OFF_S:OFF_S + SSM_WIDTH])
    ys = []
    for j in range(SLABS):
        lanes = slice(j * SLAB_STATE, (j + 1) * SLAB_STATE)
        s_j = s[:, j * LANES:(j + 1) * LANES]
        bu = _dot(s_j.astype(BF16), bmat[j])
        a_re, a_im = are[0:1, lanes], aim[0:1, lanes]
        p_re, p_im = h0re[:, lanes], h0im[:, lanes]
        n_re = a_re * p_re - a_im * p_im + bu[:, 0:SLAB_STATE]
        n_im = a_re * p_im + a_im * p_re + bu[:, SLAB_STATE:]
        hre_o[:, lanes] = n_re
        him_o[:, lanes] = n_im
        h = jnp.concatenate([n_re, n_im], axis=1).astype(BF16)
        ys.append(_dot(h, cmat[j]) + dsk[j] * s_j)
    pb = _dot(jnp.concatenate(ys, axis=1).astype(BF16), w_b[...])
    yb = pb[:, :D_MODEL] * jax.nn.sigmoid(pb[:, D_MODEL:])
    ga = jax.nn.sigmoid(_dot(hn, w_in[:, OFF_GA:OFF_GA + D_MODEL]))
    gb = jax.nn.sigmoid(_dot(hn, w_in[:, OFF_GB:OFF_GB + D_MODEL]))
    merged = (ga * ya + gb * yb).astype(BF16)
    x1_ref[...] = x + _dot(merged, w_out[...])


def _mixer_sample(x, h0re, h0im, gmix, w_in, lng, lnb, spw, spb, are, aim, bmat, cmat, dsk,
                  w_a, w_b, w_out):
    n, d = x.shape
    return pl.pallas_call(
        _mixer_sample_kernel,
        out_shape=(jax.ShapeDtypeStruct((n, d), F32),
                   jax.ShapeDtypeStruct((n, STATE_W), F32),
                   jax.ShapeDtypeStruct((n, STATE_W), F32),
                   jax.ShapeDtypeStruct((n, d), F32)),
        compiler_params=pltpu.CompilerParams(vmem_limit_bytes=VMEM_LIMIT),
        name="mixer_sample",
    )(x, h0re, h0im, gmix, w_in, lng, lnb, spw, spb, are, aim, bmat, cmat, dsk, w_a, w_b, w_out)


def _ffn_kernel(x_ref, gffn, w_gate, w_up, w_down, gfin, y_ref):
    x = x_ref[...]
    h = _rmsnorm(x, gffn[...]).astype(BF16)
    act = (jax.nn.silu(_dot(h, w_gate[...])) * _dot(h, w_up[...])).astype(BF16)
    x2 = x + _dot(act, w_down[...])
    y_ref[...] = _rmsnorm(x2, gfin[...])


def _ffn(x, gffn, w_gate, w_up, w_down, gfin, block_rows):
    n, d = x.shape
    consts = (gffn, w_gate, w_up, w_down, gfin)
    row_spec = pl.BlockSpec((block_rows, d), lambda i: (i, 0))
    return pl.pallas_call(
        _ffn_kernel,
        grid=(n // block_rows,),
        in_specs=[row_spec] + [_const_spec(c.shape) for c in consts],
        out_specs=row_spec,
        out_shape=jax.ShapeDtypeStruct((n, d), F32),
        compiler_params=pltpu.CompilerParams(dimension_semantics=("arbitrary",),
                                             vmem_limit_bytes=VMEM_LIMIT),
        name="ffn",
    )(x, *consts)


def kernel(x_prompt, x_sample, state_ssm_re, state_ssm_im, norm_mix_g, w_in, ln_v_g, ln_v_b,
           w_spatial, b_spatial, ssm_lam_re, ssm_lam_im, ssm_log_dt, ssm_b_re, ssm_b_im,
           ssm_c_re, ssm_c_im, ssm_d, w_branch_a, w_branch_b, w_out, norm_ffn_g,
           w_gate_ffn, w_up_ffn, w_down_ffn, norm_final_g):
    depth = w_in.shape[0]
    assert depth == 1
    nb, seq, d = x_prompt.shape
    ns = x_sample.shape[0]
    row = lambda a: a.reshape(1, -1)

    xp = x_prompt
    xs = x_sample.reshape(ns, d)
    gfin = row(norm_final_g)
    p_re, p_im, s_re, s_im, s_v = [], [], [], [], []
    for l in range(depth):
        are, aim, bbre, bbim = _ssm_prep(ssm_lam_re[l], ssm_lam_im[l], ssm_log_dt[l],
                                         ssm_b_re[l], ssm_b_im[l])
        bmat, cmat = _ssm_matrices(bbre, bbim, ssm_c_re[l], ssm_c_im[l])
        dsk = ssm_d[l].reshape(SLABS, 1, LANES)
        tril = jnp.tril(jnp.ones((CHUNK, CHUNK), dtype=bool))
        wsp = jnp.where(tril[None], w_spatial[l], 0.0).astype(BF16)
        bsp = jnp.repeat(b_spatial[l].T, GROUP_CH, axis=1)
        spw = row(jnp.repeat(w_spatial[l][:, 0, 0], GROUP_CH))
        spb = row(jnp.repeat(b_spatial[l][:, 0], GROUP_CH))
        shared = dict(gmix=row(norm_mix_g[l]), w_in=w_in[l].astype(BF16), lng=row(ln_v_g[l]),
                      lnb=row(ln_v_b[l]))
        tail = dict(are=are, aim=aim, bmat=bmat, cmat=cmat, dsk=dsk,
                    w_a=w_branch_a[l].astype(BF16), w_b=w_branch_b[l].astype(BF16),
                    w_out=w_out[l].astype(BF16))
        ffn_w = (row(norm_ffn_g[l]), w_gate_ffn[l].astype(BF16), w_up_ffn[l].astype(BF16),
                 w_down_ffn[l].astype(BF16))

        xp, hre, him = _mixer_prompt(xp, **shared, wsp=wsp, bsp=bsp, **tail)
        p_re.append(hre.reshape(nb, SSM_GROUPS, SSM_STATE))
        p_im.append(him.reshape(nb, SSM_GROUPS, SSM_STATE))

        xs, hre, him, v_rows = _mixer_sample(
            xs, state_ssm_re[l].reshape(ns, STATE_W), state_ssm_im[l].reshape(ns, STATE_W),
            **shared, spw=spw, spb=spb, **tail)
        s_re.append(hre.reshape(ns, SSM_GROUPS, SSM_STATE))
        s_im.append(him.reshape(ns, SSM_GROUPS, SSM_STATE))
        s_v.append(v_rows.reshape(ns, 1, d))

        xp = _ffn(xp.reshape(nb * seq, d), *ffn_w, gfin, 512).reshape(nb, seq, d)
        xs = _ffn(xs, *ffn_w, gfin, ns)

    return (xp, xs.reshape(ns, 1, d), jnp.stack(p_re), jnp.stack(p_im), jnp.stack(s_re),
            jnp.stack(s_im), jnp.stack(s_v))
```

```python
import functools

import jax
import jax.numpy as jnp
from jax import lax
from jax.experimental import pallas as pl
from jax.experimental.pallas import tpu as pltpu

F32 = jnp.float32
BF16 = jnp.bfloat16

D_MODEL = 1024
CHUNK = 128
GMLP_GROUPS = 8
GROUP_CH = D_MODEL // GMLP_GROUPS
SSM_WIDTH = 512
SSM_GROUPS = 32
SSM_STATE = 64
SSM_GROUP_CH = 16
LANES = 128
SUBLANES = 8
SLABS = SSM_WIDTH // LANES
SLAB_GROUPS = LANES // SSM_GROUP_CH
SLAB_STATE = SLAB_GROUPS * SSM_STATE
STATE_W = SSM_GROUPS * SSM_STATE
IN_WIDTH = 2 * D_MODEL + SSM_WIDTH + 2 * D_MODEL
D_FF = 2816
EPS = 1e-6

OFF_U, OFF_V, OFF_S = 0, D_MODEL, 2 * D_MODEL
OFF_GA = 2 * D_MODEL + SSM_WIDTH
OFF_GB = OFF_GA + D_MODEL

ROW_BLOCK = 256
SCAN_STEPS = 16
VMEM_LIMIT = 56 * 1024 * 1024


def _dot(a, b):
    return jnp.dot(a, b, preferred_element_type=F32)


def _rmsnorm(x, g):
    return x * lax.rsqrt(jnp.mean(x * x, axis=-1, keepdims=True) + EPS) * g


def _layernorm(x, g, b):
    mu = jnp.mean(x, axis=-1, keepdims=True)
    xc = x - mu
    var = jnp.mean(xc * xc, axis=-1, keepdims=True)
    return xc * lax.rsqrt(var + EPS) * g + b


def _const_spec(shape):
    zeros = (0,) * len(shape)
    return pl.BlockSpec(shape, lambda *_: zeros, pipeline_mode=pl.Buffered(1))


def _ssm_prep_kernel(lre_s, lim_s, ldt_s, lre_b, lim_b, ldt_b, bre, bim,
                     are_o, aim_o, bbre_o, bbim_o):
    def lam_bar(lre, lim, ldt):
        dt = jnp.exp(ldt)
        mag = jnp.exp(lre * dt)
        return mag * jnp.cos(lim * dt), mag * jnp.sin(lim * dt)

    are, aim = lam_bar(lre_s[...], lim_s[...], ldt_s[...])
    are_o[...] = jnp.broadcast_to(are, are_o.shape)
    aim_o[...] = jnp.broadcast_to(aim, aim_o.shape)

    lre, lim = lre_b[...], lim_b[...]
    are, aim = lam_bar(lre, lim, ldt_b[...])
    nre, nim = are - 1.0, aim
    den = lre * lre + lim * lim
    cre = (nre * lre + nim * lim) / den
    cim = (nim * lre - nre * lim) / den
    bbre_o[...] = cre * bre[...] - cim * bim[...]
    bbim_o[...] = cre * bim[...] + cim * bre[...]


def _ssm_prep(lam_re, lam_im, log_dt, b_re, b_im):
    g, p, h = b_re.shape
    flat = lambda a: a.reshape(1, g * p)
    rep = lambda a: jnp.repeat(a, h, axis=1)
    ldt_gp = jnp.broadcast_to(log_dt[:, None], (g, p))
    outs = pl.pallas_call(
        _ssm_prep_kernel,
        out_shape=(jax.ShapeDtypeStruct((SUBLANES, g * p), F32),
                   jax.ShapeDtypeStruct((SUBLANES, g * p), F32),
                   jax.ShapeDtypeStruct((g, p * h), F32),
                   jax.ShapeDtypeStruct((g, p * h), F32)),
        name="ssm_prep",
    )(flat(lam_re), flat(lam_im), flat(ldt_gp), rep(lam_re), rep(lam_im), rep(ldt_gp),
      b_re.reshape(g, p * h), b_im.reshape(g, p * h))
    are, aim, bbre, bbim = outs
    return are, aim, bbre.reshape(g, p, h), bbim.reshape(g, p, h)


def _ssm_matrices(bbar_re, bbar_im, c_re, c_im):
    eye = jnp.eye(SLAB_GROUPS, dtype=F32)

    def b_side(bb):
        t = bb.reshape(SLABS, SLAB_GROUPS, SSM_STATE, SSM_GROUP_CH).transpose(0, 1, 3, 2)
        return jnp.einsum('jkhp,kl->jkhlp', t, eye).reshape(SLABS, LANES, SLAB_STATE)

    def c_side(c):
        t = c.reshape(SLABS, SLAB_GROUPS, SSM_GROUP_CH, SSM_STATE).transpose(0, 1, 3, 2)
        return jnp.einsum('jkph,kl->jkplh', t, eye).reshape(SLABS, SLAB_STATE, LANES)

    bmat = jnp.concatenate([b_side(bbar_re), b_side(bbar_im)], axis=2)
    cmat = jnp.concatenate([c_side(c_re), -c_side(c_im)], axis=1)
    return bmat.astype(BF16), cmat.astype(BF16)


def _mixer_prompt_kernel(x_ref, gmix, w_in, lng, lnb, wsp, bsp, are, aim, bmat, cmat, dsk,
                         w_a, w_b, w_out,
                         x1_ref, hre_o, him_o,
                         s_tm, ys_tm, ya_s, hn_s, bu, hbuf, st_re, st_im):
    i = pl.program_id(0)
    nb = x_ref.shape[0]
    rows = nb * CHUNK
    per_block = ROW_BLOCK // CHUNK

    @pl.when(i == 0)
    def _():
        st_re[...] = jnp.zeros_like(st_re)
        st_im[...] = jnp.zeros_like(st_im)

    def phase1(blk, carry):
        r0 = pl.multiple_of(blk * ROW_BLOCK, ROW_BLOCK)
        x = jnp.concatenate([x_ref[blk * per_block + k] for k in range(per_block)], axis=0)
        hn = _rmsnorm(x, gmix[...]).astype(BF16)
        hn_s[pl.ds(r0, ROW_BLOCK), :] = hn
        u = jax.nn.gelu(_dot(hn, w_in[:, OFF_U:OFF_U + D_MODEL]))
        v = _layernorm(jax.nn.gelu(_dot(hn, w_in[:, OFF_V:OFF_V + D_MODEL])), lng[...], lnb[...])
        vb = v.astype(BF16)
        mixed = []
        for k in range(per_block):
            cols = [_dot(wsp[g], vb[k * CHUNK:(k + 1) * CHUNK, g * GROUP_CH:(g + 1) * GROUP_CH])
                    for g in range(GMLP_GROUPS)]
            mixed.append(jnp.concatenate(cols, axis=1) + bsp[...])
        gated = (u * jnp.concatenate(mixed, axis=0)).astype(BF16)
        ya_s[pl.ds(r0, ROW_BLOCK), :] = _dot(gated, w_a[...])
        s = _dot(hn, w_in[:, OFF_S:OFF_S + SSM_WIDTH])
        for k in range(per_block):
            b = blk * per_block + k
            for j in range(SLABS):
                s_tm[j, pl.ds(b, CHUNK, stride=nb), :] = s[k * CHUNK:(k + 1) * CHUNK,
                                                          j * LANES:(j + 1) * LANES]
        return carry

    lax.fori_loop(0, rows // ROW_BLOCK, phase1, 0, unroll=2)

    sub_rows = SCAN_STEPS * nb

    def project_in(q):
        for j in range(SLABS):
            s_blk = s_tm[j, q * sub_rows:(q + 1) * sub_rows, :]
            bu[q % 2, j] = _dot(s_blk.astype(BF16), bmat[j])

    def recur(q):
        for j in range(SLABS):
            lanes = slice(j * SLAB_STATE, (j + 1) * SLAB_STATE)
            a_re, a_im = are[:, lanes], aim[:, lanes]
            h_re, h_im = st_re[:, lanes], st_im[:, lanes]
            for t in range(0, SCAN_STEPS, 2):
                pair = []
                for r in (slice(t * nb, (t + 1) * nb), slice((t + 1) * nb, (t + 2) * nb)):
                    n_re = a_re * h_re - a_im * h_im + bu[q % 2, j, r, 0:SLAB_STATE]
                    n_im = a_re * h_im + a_im * h_re + bu[q % 2, j, r, SLAB_STATE:2 * SLAB_STATE]
                    h_re, h_im = n_re, n_im
                    pair.append(jnp.concatenate([n_re, n_im], axis=1))
                hbuf[q % 2, j, t * nb:(t + 2) * nb, :] = jnp.concatenate(pair, axis=0).astype(BF16)
            st_re[:, lanes] = h_re
            st_im[:, lanes] = h_im

    def project_out(q):
        for j in range(SLABS):
            r = slice(q * sub_rows, (q + 1) * sub_rows)
            ys_tm[j, r, :] = _dot(hbuf[q % 2, j], cmat[j]) + dsk[j] * s_tm[j, r, :]

    n_sub = rows // sub_rows
    project_in(0)
    for q in range(n_sub):
        if q + 1 < n_sub:
            project_in(q + 1)
        recur(q)
        if q >= 1:
            project_out(q - 1)
    project_out(n_sub - 1)

    def phase3(blk, carry):
        r0 = pl.multiple_of(blk * ROW_BLOCK, ROW_BLOCK)
        ys = jnp.concatenate(
            [jnp.concatenate([ys_tm[j, pl.ds(blk * per_block + k, CHUNK, stride=nb), :]
                              for j in range(SLABS)], axis=1)
             for k in range(per_block)], axis=0)
        pb = _dot(ys.astype(BF16), w_b[...])
        yb = pb[:, :D_MODEL] * jax.nn.sigmoid(pb[:, D_MODEL:])
        hn = hn_s[pl.ds(r0, ROW_BLOCK), :]
        ga = jax.nn.sigmoid(_dot(hn, w_in[:, OFF_GA:OFF_GA + D_MODEL]))
        gb = jax.nn.sigmoid(_dot(hn, w_in[:, OFF_GB:OFF_GB + D_MODEL]))
        merged = (ga * ya_s[pl.ds(r0, ROW_BLOCK), :] + gb * yb).astype(BF16)
        y = _dot(merged, w_out[...])
        for k in range(per_block):
            b = blk * per_block + k
            x1_ref[b] = x_ref[b] + y[k * CHUNK:(k + 1) * CHUNK]
        return carry

    lax.fori_loop(0, rows // ROW_BLOCK, phase3, 0, unroll=2)

    @pl.when(i == pl.num_programs(0) - 1)
    def _():
        hre_o[...] = st_re[...]
        him_o[...] = st_im[...]


def _mixer_prompt(x, gmix, w_in, lng, lnb, wsp, bsp, are, aim, bmat, cmat, dsk, w_a, w_b, w_out):
    nb, seq, d = x.shape
    rows = nb * CHUNK
    consts = (gmix, w_in, lng, lnb, wsp, bsp, are, aim, bmat, cmat, dsk, w_a, w_b, w_out)
    x_spec = pl.BlockSpec((nb, CHUNK, d), lambda i: (0, i, 0))
    st_spec = pl.BlockSpec((nb, STATE_W), lambda i: (0, 0))
    return pl.pallas_call(
        _mixer_prompt_kernel,
        grid=(seq // CHUNK,),
        in_specs=[x_spec] + [_const_spec(c.shape) for c in consts],
        out_specs=(x_spec, st_spec, st_spec),
        out_shape=(jax.ShapeDtypeStruct(x.shape, F32),
                   jax.ShapeDtypeStruct((nb, STATE_W), F32),
                   jax.ShapeDtypeStruct((nb, STATE_W), F32)),
        scratch_shapes=[
            pltpu.VMEM((SLABS, rows, LANES), F32),
            pltpu.VMEM((SLABS, rows, LANES), F32),
            pltpu.VMEM((rows, d), F32),
            pltpu.VMEM((rows, d), BF16),
            pltpu.VMEM((2, SLABS, SCAN_STEPS * nb, 2 * SLAB_STATE), F32),
            pltpu.VMEM((2, SLABS, SCAN_STEPS * nb, 2 * SLAB_STATE), BF16),
            pltpu.VMEM((nb, STATE_W), F32),
            pltpu.VMEM((nb, STATE_W), F32),
        ],
        compiler_params=pltpu.CompilerParams(dimension_semantics=("arbitrary",),
                                             vmem_limit_bytes=VMEM_LIMIT),
        name="mixer_prompt",
    )(x, *consts)


def _mixer_sample_kernel(x_ref, h0re, h0im, gmix, w_in, lng, lnb, spw, spb, are, aim, bmat, cmat,
                         dsk, w_a, w_b, w_out,
                         x1_ref, hre_o, him_o, v_o):
    x = x_ref[...]
    hn = _rmsnorm(x, gmix[...]).astype(BF16)
    u = jax.nn.gelu(_dot(hn, w_in[:, OFF_U:OFF_U + D_MODEL]))
    v = _layernorm(jax.nn.gelu(_dot(hn, w_in[:, OFF_V:OFF_V + D_MODEL])), lng[...], lnb[...])
    v_o[...] = v
    mixed = v * spw[...] + spb[...]
    ya = _dot((u * mixed).astype(BF16), w_a[...])
    s = _dot(hn, w_in[:, OFF_S:OFF_S + SSM_WIDTH])
    ys = []
    for j in range(SLABS):
        lanes = slice(j * SLAB_STATE, (j + 1) * SLAB_STATE)
        s_j = s[:, j * LANES:(j + 1) * LANES]
        bu = _dot(s_j.astype(BF16), bmat[j])
        a_re, a_im = are[0:1, lanes], aim[0:1, lanes]
        p_re, p_im = h0re[:, lanes], h0im[:, lanes]
        n_re = a_re * p_re - a_im * p_im + bu[:, 0:SLAB_STATE]
        n_im = a_re * p_im + a_im * p_re + bu[:, SLAB_STATE:]
        hre_o[:, lanes] = n_re
        him_o[:, lanes] = n_im
        h = jnp.concatenate([n_re, n_im], axis=1).astype(BF16)
        ys.append(_dot(h, cmat[j]) + dsk[j] * s_j)
    pb = _dot(jnp.concatenate(ys, axis=1).astype(BF16), w_b[...])
    yb = pb[:, :D_MODEL] * jax.nn.sigmoid(pb[:, D_MODEL:])
    ga = jax.nn.sigmoid(_dot(hn, w_in[:, OFF_GA:OFF_GA + D_MODEL]))
    gb = jax.nn.sigmoid(_dot(hn, w_in[:, OFF_GB:OFF_GB + D_MODEL]))
    merged = (ga * ya + gb * yb).astype(BF16)
    x1_ref[...] = x + _dot(merged, w_out[...])


def _mixer_sample(x, h0re, h0im, gmix, w_in, lng, lnb, spw, spb, are, aim, bmat, cmat, dsk,
                  w_a, w_b, w_out):
    n, d = x.shape
    return pl.pallas_call(
        _mixer_sample_kernel,
        out_shape=(jax.ShapeDtypeStruct((n, d), F32),
                   jax.ShapeDtypeStruct((n, STATE_W), F32),
                   jax.ShapeDtypeStruct((n, STATE_W), F32),
                   jax.ShapeDtypeStruct((n, d), F32)),
        compiler_params=pltpu.CompilerParams(vmem_limit_bytes=VMEM_LIMIT),
        name="mixer_sample",
    )(x, h0re, h0im, gmix, w_in, lng, lnb, spw, spb, are, aim, bmat, cmat, dsk, w_a, w_b, w_out)


def _ffn_kernel(x_ref, gffn, w_gate, w_up, w_down, gfin, y_ref):
    x = x_ref[...]
    h = _rmsnorm(x, gffn[...]).astype(BF16)
    act = (jax.nn.silu(_dot(h, w_gate[...])) * _dot(h, w_up[...])).astype(BF16)
    x2 = x + _dot(act, w_down[...])
    y_ref[...] = _rmsnorm(x2, gfin[...])


def _ffn(x, gffn, w_gate, w_up, w_down, gfin, block_rows):
    n, d = x.shape
    consts = (gffn, w_gate, w_up, w_down, gfin)
    row_spec = pl.BlockSpec((block_rows, d), lambda i: (i, 0))
    return pl.pallas_call(
        _ffn_kernel,
        grid=(n // block_rows,),
        in_specs=[row_spec] + [_const_spec(c.shape) for c in consts],
        out_specs=row_spec,
        out_shape=jax.ShapeDtypeStruct((n, d), F32),
        compiler_params=pltpu.CompilerParams(dimension_semantics=("arbitrary",),
                                             vmem_limit_bytes=VMEM_LIMIT),
        name="ffn",
    )(x, *consts)


def kernel(x_prompt, x_sample, state_ssm_re, state_ssm_im, norm_mix_g, w_in, ln_v_g, ln_v_b,
           w_spatial, b_spatial, ssm_lam_re, ssm_lam_im, ssm_log_dt, ssm_b_re, ssm_b_im,
           ssm_c_re, ssm_c_im, ssm_d, w_branch_a, w_branch_b, w_out, norm_ffn_g,
           w_gate_ffn, w_up_ffn, w_down_ffn, norm_final_g):
    depth = w_in.shape[0]
    assert depth == 1
    nb, seq, d = x_prompt.shape
    ns = x_sample.shape[0]
    row = lambda a: a.reshape(1, -1)

    xp = x_prompt
    xs = x_sample.reshape(ns, d)
    gfin = row(norm_final_g)
    p_re, p_im, s_re, s_im, s_v = [], [], [], [], []
    for l in range(depth):
        are, aim, bbre, bbim = _ssm_prep(ssm_lam_re[l], ssm_lam_im[l], ssm_log_dt[l],
                                         ssm_b_re[l], ssm_b_im[l])
        bmat, cmat = _ssm_matrices(bbre, bbim, ssm_c_re[l], ssm_c_im[l])
        dsk = ssm_d[l].reshape(SLABS, 1, LANES)
        tril = jnp.tril(jnp.ones((CHUNK, CHUNK), dtype=bool))
        wsp = jnp.where(tril[None], w_spatial[l], 0.0).astype(BF16)
        bsp = jnp.repeat(b_spatial[l].T, GROUP_CH, axis=1)
        spw = row(jnp.repeat(w_spatial[l][:, 0, 0], GROUP_CH))
        spb = row(jnp.repeat(b_spatial[l][:, 0], GROUP_CH))
        shared = dict(gmix=row(norm_mix_g[l]), w_in=w_in[l].astype(BF16), lng=row(ln_v_g[l]),
                      lnb=row(ln_v_b[l]))
        tail = dict(are=are, aim=aim, bmat=bmat, cmat=cmat, dsk=dsk,
                    w_a=w_branch_a[l].astype(BF16), w_b=w_branch_b[l].astype(BF16),
                    w_out=w_out[l].astype(BF16))
        ffn_w = (row(norm_ffn_g[l]), w_gate_ffn[l].astype(BF16), w_up_ffn[l].astype(BF16),
                 w_down_ffn[l].astype(BF16))

        xp, hre, him = _mixer_prompt(xp, **shared, wsp=wsp, bsp=bsp, **tail)
        p_re.append(hre.reshape(nb, SSM_GROUPS, SSM_STATE))
        p_im.append(him.reshape(nb, SSM_GROUPS, SSM_STATE))

        xs, hre, him, v_rows = _mixer_sample(
            xs, state_ssm_re[l].reshape(ns, STATE_W), state_ssm_im[l].reshape(ns, STATE_W),
            **shared, spw=spw, spb=spb, **tail)
        s_re.append(hre.reshape(ns, SSM_GROUPS, SSM_STATE))
        s_im.append(him.reshape(ns, SSM_GROUPS, SSM_STATE))
        s_v.append(v_rows.reshape(ns, 1, d))

        xp = _ffn(xp.reshape(nb * seq, d), *ffn_w, gfin, 512).reshape(nb, seq, d)
        xs = _ffn(xs, *ffn_w, gfin, ns)

    return (xp, xs.reshape(ns, 1, d), jnp.stack(p_re), jnp.stack(p_im), jnp.stack(s_re),
            jnp.stack(s_im), jnp.stack(s_v))
```

```python
import functools

import jax
import jax.numpy as jnp
from jax import lax
from jax.experimental import pallas as pl
from jax.experimental.pallas import tpu as pltpu

F32 = jnp.float32
BF16 = jnp.bfloat16

D_MODEL = 1024
CHUNK = 128
GMLP_GROUPS = 8
GROUP_CH = D_MODEL // GMLP_GROUPS
SSM_WIDTH = 512
SSM_GROUPS = 32
SSM_STATE = 64
SSM_GROUP_CH = 16
LANES = 128
SUBLANES = 8
SLABS = SSM_WIDTH // LANES
SLAB_GROUPS = LANES // SSM_GROUP_CH
SLAB_STATE = SLAB_GROUPS * SSM_STATE
STATE_W = SSM_GROUPS * SSM_STATE
IN_WIDTH = 2 * D_MODEL + SSM_WIDTH + 2 * D_MODEL
D_FF = 2816
EPS = 1e-6

OFF_U, OFF_V, OFF_S = 0, D_MODEL, 2 * D_MODEL
OFF_GA = 2 * D_MODEL + SSM_WIDTH
OFF_GB = OFF_GA + D_MODEL

ROW_BLOCK = 256
SCAN_STEPS = 16
VMEM_LIMIT = 56 * 1024 * 1024


def _dot(a, b):
    return jnp.dot(a, b, preferred_element_type=F32)


def _rmsnorm(x, g):
    return x * lax.rsqrt(jnp.mean(x * x, axis=-1, keepdims=True) + EPS) * g


def _layernorm(x, g, b):
    mu = jnp.mean(x, axis=-1, keepdims=True)
    xc = x - mu
    var = jnp.mean(xc * xc, axis=-1, keepdims=True)
    return xc * lax.rsqrt(var + EPS) * g + b


def _const_spec(shape):
    zeros = (0,) * len(shape)
    return pl.BlockSpec(shape, lambda *_: zeros, pipeline_mode=pl.Buffered(1))


def _ssm_prep_kernel(lre_s, lim_s, ldt_s, lre_b, lim_b, ldt_b, bre, bim,
                     are_o, aim_o, bbre_o, bbim_o):
    def lam_bar(lre, lim, ldt):
        dt = jnp.exp(ldt)
        mag = jnp.exp(lre * dt)
        return mag * jnp.cos(lim * dt), mag * jnp.sin(lim * dt)

    are, aim = lam_bar(lre_s[...], lim_s[...], ldt_s[...])
    are_o[...] = jnp.broadcast_to(are, are_o.shape)
    aim_o[...] = jnp.broadcast_to(aim, aim_o.shape)

    lre, lim = lre_b[...], lim_b[...]
    are, aim = lam_bar(lre, lim, ldt_b[...])
    nre, nim = are - 1.0, aim
    den = lre * lre + lim * lim
    cre = (nre * lre + nim * lim) / den
    cim = (nim * lre - nre * lim) / den
    bbre_o[...] = cre * bre[...] - cim * bim[...]
    bbim_o[...] = cre * bim[...] + cim * bre[...]


def _ssm_prep(lam_re, lam_im, log_dt, b_re, b_im):
    g, p, h = b_re.shape
    flat = lambda a: a.reshape(1, g * p)
    rep = lambda a: jnp.repeat(a, h, axis=1)
    ldt_gp = jnp.broadcast_to(log_dt[:, None], (g, p))
    outs = pl.pallas_call(
        _ssm_prep_kernel,
        out_shape=(jax.ShapeDtypeStruct((SUBLANES, g * p), F32),
                   jax.ShapeDtypeStruct((SUBLANES, g * p), F32),
                   jax.ShapeDtypeStruct((g, p * h), F32),
                   jax.ShapeDtypeStruct((g, p * h), F32)),
        name="ssm_prep",
    )(flat(lam_re), flat(lam_im), flat(ldt_gp), rep(lam_re), rep(lam_im), rep(ldt_gp),
      b_re.reshape(g, p * h), b_im.reshape(g, p * h))
    are, aim, bbre, bbim = outs
    return are, aim, bbre.reshape(g, p, h), bbim.reshape(g, p, h)


def _ssm_matrices(bbar_re, bbar_im, c_re, c_im):
    eye = jnp.eye(SLAB_GROUPS, dtype=F32)

    def b_side(bb):
        t = bb.reshape(SLABS, SLAB_GROUPS, SSM_STATE, SSM_GROUP_CH).transpose(0, 1, 3, 2)
        return jnp.einsum('jkhp,kl->jkhlp', t, eye).reshape(SLABS, LANES, SLAB_STATE)

    def c_side(c):
        t = c.reshape(SLABS, SLAB_GROUPS, SSM_GROUP_CH, SSM_STATE).transpose(0, 1, 3, 2)
        return jnp.einsum('jkph,kl->jkplh', t, eye).reshape(SLABS, SLAB_STATE, LANES)

    bmat = jnp.concatenate([b_side(bbar_re), b_side(bbar_im)], axis=2)
    cmat = jnp.concatenate([c_side(c_re), -c_side(c_im)], axis=1)
    return bmat.astype(BF16), cmat.astype(BF16)


def _mixer_prompt_kernel(x_ref, gmix, w_in, lng, lnb, wsp, bsp, are, aim, bmat, cmat, dsk,
                         w_a, w_b, w_out,
                         x1_ref, hre_o, him_o,
                         s_tm, ys_tm, ya_s, hn_s, bu, hbuf, st_re, st_im):
    i = pl.program_id(0)
    nb = x_ref.shape[0]
    rows = nb * CHUNK
    per_block = ROW_BLOCK // CHUNK

    @pl.when(i == 0)
    def _():
        st_re[...] = jnp.zeros_like(st_re)
        st_im[...] = jnp.zeros_like(st_im)

    def project(blk):
        r = slice(blk * ROW_BLOCK, (blk + 1) * ROW_BLOCK)
        x = jnp.concatenate([x_ref[blk * per_block + k] for k in range(per_block)], axis=0)
        hn = _rmsnorm(x, gmix[...]).astype(BF16)
        hn_s[r, :] = hn
        zu = _dot(hn, w_in[:, OFF_U:OFF_U + D_MODEL])
        zv = _dot(hn, w_in[:, OFF_V:OFF_V + D_MODEL])
        s = _dot(hn, w_in[:, OFF_S:OFF_S + SSM_WIDTH])
        for k in range(per_block):
            b = blk * per_block + k
            for j in range(SLABS):
                s_tm[j, pl.ds(b, CHUNK, stride=nb), :] = s[k * CHUNK:(k + 1) * CHUNK,
                                                          j * LANES:(j + 1) * LANES]
        return zu, zv

    def gate(blk, zu, zv):
        u = jax.nn.gelu(zu)
        vb = _layernorm(jax.nn.gelu(zv), lng[...], lnb[...]).astype(BF16)
        mixed = []
        for k in range(per_block):
            cols = [_dot(wsp[g], vb[k * CHUNK:(k + 1) * CHUNK, g * GROUP_CH:(g + 1) * GROUP_CH])
                    for g in range(GMLP_GROUPS)]
            mixed.append(jnp.concatenate(cols, axis=1) + bsp[...])
        gated = (u * jnp.concatenate(mixed, axis=0)).astype(BF16)
        ya_s[blk * ROW_BLOCK:(blk + 1) * ROW_BLOCK, :] = _dot(gated, w_a[...])

    n_blk = rows // ROW_BLOCK
    z = project(0)
    for blk in range(n_blk):
        z_next = project(blk + 1) if blk + 1 < n_blk else None
        gate(blk, *z)
        z = z_next

    sub_rows = SCAN_STEPS * nb

    def project_in(q):
        for j in range(SLABS):
            s_blk = s_tm[j, q * sub_rows:(q + 1) * sub_rows, :]
            bu[q % 2, j] = _dot(s_blk.astype(BF16), bmat[j])

    def recur(q):
        for j in range(SLABS):
            lanes = slice(j * SLAB_STATE, (j + 1) * SLAB_STATE)
            a_re, a_im = are[:, lanes], aim[:, lanes]
            h_re, h_im = st_re[:, lanes], st_im[:, lanes]
            for t in range(0, SCAN_STEPS, 2):
                pair = []
                for r in (slice(t * nb, (t + 1) * nb), slice((t + 1) * nb, (t + 2) * nb)):
                    n_re = a_re * h_re - a_im * h_im + bu[q % 2, j, r, 0:SLAB_STATE]
                    n_im = a_re * h_im + a_im * h_re + bu[q % 2, j, r, SLAB_STATE:2 * SLAB_STATE]
                    h_re, h_im = n_re, n_im
                    pair.append(jnp.concatenate([n_re, n_im], axis=1))
                hbuf[q % 2, j, t * nb:(t + 2) * nb, :] = jnp.concatenate(pair, axis=0).astype(BF16)
            st_re[:, lanes] = h_re
            st_im[:, lanes] = h_im

    def project_out(q):
        for j in range(SLABS):
            r = slice(q * sub_rows, (q + 1) * sub_rows)
            ys_tm[j, r, :] = _dot(hbuf[q % 2, j], cmat[j]) + dsk[j] * s_tm[j, r, :]

    n_sub = rows // sub_rows
    project_in(0)
    for q in range(n_sub):
        if q + 1 < n_sub:
            project_in(q + 1)
        recur(q)
        if q >= 1:
            project_out(q - 1)
    project_out(n_sub - 1)

    def phase3(blk, carry):
        r0 = pl.multiple_of(blk * ROW_BLOCK, ROW_BLOCK)
        ys = jnp.concatenate(
            [jnp.concatenate([ys_tm[j, pl.ds(blk * per_block + k, CHUNK, stride=nb), :]
                              for j in range(SLABS)], axis=1)
             for k in range(per_block)], axis=0)
        pb = _dot(ys.astype(BF16), w_b[...])
        yb = pb[:, :D_MODEL] * jax.nn.sigmoid(pb[:, D_MODEL:])
        hn = hn_s[pl.ds(r0, ROW_BLOCK), :]
        ga = jax.nn.sigmoid(_dot(hn, w_in[:, OFF_GA:OFF_GA + D_MODEL]))
        gb = jax.nn.sigmoid(_dot(hn, w_in[:, OFF_GB:OFF_GB + D_MODEL]))
        merged = (ga * ya_s[pl.ds(r0, ROW_BLOCK), :] + gb * yb).astype(BF16)
        y = _dot(merged, w_out[...])
        for k in range(per_block):
            b = blk * per_block + k
            x1_ref[b] = x_ref[b] + y[k * CHUNK:(k + 1) * CHUNK]
        return carry

    lax.fori_loop(0, rows // ROW_BLOCK, phase3, 0, unroll=2)

    @pl.when(i == pl.num_programs(0) - 1)
    def _():
        hre_o[...] = st_re[...]
        him_o[...] = st_im[...]


def _mixer_prompt(x, gmix, w_in, lng, lnb, wsp, bsp, are, aim, bmat, cmat, dsk, w_a, w_b, w_out):
    nb, seq, d = x.shape
    rows = nb * CHUNK
    consts = (gmix, w_in, lng, lnb, wsp, bsp, are, aim, bmat, cmat, dsk, w_a, w_b, w_out)
    x_spec = pl.BlockSpec((nb, CHUNK, d), lambda i: (0, i, 0))
    st_spec = pl.BlockSpec((nb, STATE_W), lambda i: (0, 0))
    return pl.pallas_call(
        _mixer_prompt_kernel,
        grid=(seq // CHUNK,),
        in_specs=[x_spec] + [_const_spec(c.shape) for c in consts],
        out_specs=(x_spec, st_spec, st_spec),
        out_shape=(jax.ShapeDtypeStruct(x.shape, F32),
                   jax.ShapeDtypeStruct((nb, STATE_W), F32),
                   jax.ShapeDtypeStruct((nb, STATE_W), F32)),
        scratch_shapes=[
            pltpu.VMEM((SLABS, rows, LANES), F32),
            pltpu.VMEM((SLABS, rows, LANES), F32),
            pltpu.VMEM((rows, d), F32),
            pltpu.VMEM((rows, d), BF16),
            pltpu.VMEM((2, SLABS, SCAN_STEPS * nb, 2 * SLAB_STATE), F32),
            pltpu.VMEM((2, SLABS, SCAN_STEPS * nb, 2 * SLAB_STATE), BF16),
            pltpu.VMEM((nb, STATE_W), F32),
            pltpu.VMEM((nb, STATE_W), F32),
        ],
        compiler_params=pltpu.CompilerParams(dimension_semantics=("arbitrary",),
                                             vmem_limit_bytes=VMEM_LIMIT),
        name="mixer_prompt",
    )(x, *consts)


def _mixer_sample_kernel(x_ref, h0re, h0im, gmix, w_in, lng, lnb, spw, spb, are, aim, bmat, cmat,
                         dsk, w_a, w_b, w_out,
                         x1_ref, hre_o, him_o, v_o):
    x = x_ref[...]
    hn = _rmsnorm(x, gmix[...]).astype(BF16)
    u = jax.nn.gelu(_dot(hn, w_in[:, OFF_U:OFF_U + D_MODEL]))
    v = _layernorm(jax.nn.gelu(_dot(hn, w_in[:, OFF_V:OFF_V + D_MODEL])), lng[...], lnb[...])
    v_o[...] = v
    mixed = v * spw[...] + spb[...]
    ya = _dot((u * mixed).astype(BF16), w_a[...])
    s = _dot(hn, w_in[:, OFF_S:OFF_S + SSM_WIDTH])
    ys = []
    for j in range(SLABS):
        lanes = slice(j * SLAB_STATE, (j + 1) * SLAB_STATE)
        s_j = s[:, j * LANES:(j + 1) * LANES]
        bu = _dot(s_j.astype(BF16), bmat[j])
        a_re, a_im = are[0:1, lanes], aim[0:1, lanes]
        p_re, p_im = h0re[:, lanes], h0im[:, lanes]
        n_re = a_re * p_re - a_im * p_im + bu[:, 0:SLAB_STATE]
        n_im = a_re * p_im + a_im * p_re + bu[:, SLAB_STATE:]
        hre_o[:, lanes] = n_re
        him_o[:, lanes] = n_im
        h = jnp.concatenate([n_re, n_im], axis=1).astype(BF16)
        ys.append(_dot(h, cmat[j]) + dsk[j] * s_j)
    pb = _dot(jnp.concatenate(ys, axis=1).astype(BF16), w_b[...])
    yb = pb[:, :D_MODEL] * jax.nn.sigmoid(pb[:, D_MODEL:])
    ga = jax.nn.sigmoid(_dot(hn, w_in[:, OFF_GA:OFF_GA + D_MODEL]))
    gb = jax.nn.sigmoid(_dot(hn, w_in[:, OFF_GB:OFF_GB + D_MODEL]))
    merged = (ga * ya + gb * yb).astype(BF16)
    x1_ref[...] = x + _dot(merged, w_out[...])


def _mixer_sample(x, h0re, h0im, gmix, w_in, lng, lnb, spw, spb, are, aim, bmat, cmat, dsk,
                  w_a, w_b, w_out):
    n, d = x.shape
    return pl.pallas_call(
        _mixer_sample_kernel,
        out_shape=(jax.ShapeDtypeStruct((n, d), F32),
                   jax.ShapeDtypeStruct((n, STATE_W), F32),
                   jax.ShapeDtypeStruct((n, STATE_W), F32),
                   jax.ShapeDtypeStruct((n, d), F32)),
        compiler_params=pltpu.CompilerParams(vmem_limit_bytes=VMEM_LIMIT),
        name="mixer_sample",
    )(x, h0re, h0im, gmix, w_in, lng, lnb, spw, spb, are, aim, bmat, cmat, dsk, w_a, w_b, w_out)


def _ffn_kernel(x_ref, gffn, w_gate, w_up, w_down, gfin, y_ref):
    x = x_ref[...]
    h = _rmsnorm(x, gffn[...]).astype(BF16)
    act = (jax.nn.silu(_dot(h, w_gate[...])) * _dot(h, w_up[...])).astype(BF16)
    x2 = x + _dot(act, w_down[...])
    y_ref[...] = _rmsnorm(x2, gfin[...])


def _ffn(x, gffn, w_gate, w_up, w_down, gfin, block_rows):
    n, d = x.shape
    consts = (gffn, w_gate, w_up, w_down, gfin)
    row_spec = pl.BlockSpec((block_rows, d), lambda i: (i, 0))
    return pl.pallas_call(
        _ffn_kernel,
        grid=(n // block_rows,),
        in_specs=[row_spec] + [_const_spec(c.shape) for c in consts],
        out_specs=row_spec,
        out_shape=jax.ShapeDtypeStruct((n, d), F32),
        compiler_params=pltpu.CompilerParams(dimension_semantics=("arbitrary",),
                                             vmem_limit_bytes=VMEM_LIMIT),
        name="ffn",
    )(x, *consts)


def kernel(x_prompt, x_sample, state_ssm_re, state_ssm_im, norm_mix_g, w_in, ln_v_g, ln_v_b,
           w_spatial, b_spatial, ssm_lam_re, ssm_lam_im, ssm_log_dt, ssm_b_re, ssm_b_im,
           ssm_c_re, ssm_c_im, ssm_d, w_branch_a, w_branch_b, w_out, norm_ffn_g,
           w_gate_ffn, w_up_ffn, w_down_ffn, norm_final_g):
    depth = w_in.shape[0]
    assert depth == 1
    nb, seq, d = x_prompt.shape
    ns = x_sample.shape[0]
    row = lambda a: a.reshape(1, -1)

    xp = x_prompt
    xs = x_sample.reshape(ns, d)
    gfin = row(norm_final_g)
    p_re, p_im, s_re, s_im, s_v = [], [], [], [], []
    for l in range(depth):
        are, aim, bbre, bbim = _ssm_prep(ssm_lam_re[l], ssm_lam_im[l], ssm_log_dt[l],
                                         ssm_b_re[l], ssm_b_im[l])
        bmat, cmat = _ssm_matrices(bbre, bbim, ssm_c_re[l], ssm_c_im[l])
        dsk = ssm_d[l].reshape(SLABS, 1, LANES)
        tril = jnp.tril(jnp.ones((CHUNK, CHUNK), dtype=bool))
        wsp = jnp.where(tril[None], w_spatial[l], 0.0).astype(BF16)
        bsp = jnp.repeat(b_spatial[l].T, GROUP_CH, axis=1)
        spw = row(jnp.repeat(w_spatial[l][:, 0, 0], GROUP_CH))
        spb = row(jnp.repeat(b_spatial[l][:, 0], GROUP_CH))
        shared = dict(gmix=row(norm_mix_g[l]), w_in=w_in[l].astype(BF16), lng=row(ln_v_g[l]),
                      lnb=row(ln_v_b[l]))
        tail = dict(are=are, aim=aim, bmat=bmat, cmat=cmat, dsk=dsk,
                    w_a=w_branch_a[l].astype(BF16), w_b=w_branch_b[l].astype(BF16),
                    w_out=w_out[l].astype(BF16))
        ffn_w = (row(norm_ffn_g[l]), w_gate_ffn[l].astype(BF16), w_up_ffn[l].astype(BF16),
                 w_down_ffn[l].astype(BF16))

        xp, hre, him = _mixer_prompt(xp, **shared, wsp=wsp, bsp=bsp, **tail)
        p_re.append(hre.reshape(nb, SSM_GROUPS, SSM_STATE))
        p_im.append(him.reshape(nb, SSM_GROUPS, SSM_STATE))

        xs, hre, him, v_rows = _mixer_sample(
            xs, state_ssm_re[l].reshape(ns, STATE_W), state_ssm_im[l].reshape(ns, STATE_W),
            **shared, spw=spw, spb=spb, **tail)
        s_re.append(hre.reshape(ns, SSM_GROUPS, SSM_STATE))
        s_im.append(him.reshape(ns, SSM_GROUPS, SSM_STATE))
        s_v.append(v_rows.reshape(ns, 1, d))

        xp = _ffn(xp.reshape(nb * seq, d), *ffn_w, gfin, 512).reshape(nb, seq, d)
        xs = _ffn(xs, *ffn_w, gfin, ns)

    return (xp, xs.reshape(ns, 1, d), jnp.stack(p_re), jnp.stack(p_im), jnp.stack(s_re),
            jnp.stack(s_im), jnp.stack(s_v))
```

```python
import functools

import jax
import jax.numpy as jnp
from jax import lax
from jax.experimental import pallas as pl
from jax.experimental.pallas import tpu as pltpu

F32 = jnp.float32
BF16 = jnp.bfloat16

D_MODEL = 1024
CHUNK = 128
GMLP_GROUPS = 8
GROUP_CH = D_MODEL // GMLP_GROUPS
SSM_WIDTH = 512
SSM_GROUPS = 32
SSM_STATE = 64
SSM_GROUP_CH = 16
LANES = 128
SUBLANES = 8
SLABS = SSM_WIDTH // LANES
SLAB_GROUPS = LANES // SSM_GROUP_CH
SLAB_STATE = SLAB_GROUPS * SSM_STATE
STATE_W = SSM_GROUPS * SSM_STATE
IN_WIDTH = 2 * D_MODEL + SSM_WIDTH + 2 * D_MODEL
D_FF = 2816
EPS = 1e-6

OFF_U, OFF_V, OFF_S = 0, D_MODEL, 2 * D_MODEL
OFF_GA = 2 * D_MODEL + SSM_WIDTH
OFF_GB = OFF_GA + D_MODEL

ROW_BLOCK = 256
SCAN_STEPS = 16
VMEM_LIMIT = 56 * 1024 * 1024


def _dot(a, b):
    return jnp.dot(a, b, preferred_element_type=F32)


def _rmsnorm(x, g):
    return x * lax.rsqrt(jnp.mean(x * x, axis=-1, keepdims=True) + EPS) * g


def _layernorm(x, g, b):
    mu = jnp.mean(x, axis=-1, keepdims=True)
    xc = x - mu
    var = jnp.mean(xc * xc, axis=-1, keepdims=True)
    return xc * lax.rsqrt(var + EPS) * g + b


def _const_spec(shape):
    zeros = (0,) * len(shape)
    return pl.BlockSpec(shape, lambda *_: zeros, pipeline_mode=pl.Buffered(1))


def _dot_exact(a, b):
    return jnp.dot(a, b, preferred_element_type=F32, precision=lax.Precision.HIGHEST)


def _group_mask(shape, row_div, lane_div):
    rows = lax.broadcasted_iota(jnp.int32, shape, 0) // row_div
    lanes = lax.broadcasted_iota(jnp.int32, shape, 1) // lane_div
    return rows == lanes


def _prep_kernel(lre, lim, ldt, bre, bim, cre_in, cim_in, wsp_in, bsp_in,
                 are_o, aim_o, bmat_o, cmat_o, wsp_o, bsp_o, spw_o):
    dt = jnp.exp(ldt[...])
    lr, li = lre[...], lim[...]
    mag = jnp.exp(lr * dt)
    a_re, a_im = mag * jnp.cos(li * dt), mag * jnp.sin(li * dt)
    n_re, n_im = a_re - 1.0, a_im
    den = lr * lr + li * li
    k_re = (n_re * lr + n_im * li) / den
    k_im = (n_im * lr - n_re * li) / den

    rep_h = (lax.broadcasted_iota(jnp.int32, (SSM_GROUP_CH, LANES), 1) % SSM_GROUP_CH
             == lax.broadcasted_iota(jnp.int32, (SSM_GROUP_CH, LANES), 0)).astype(F32)
    rep_p = (lax.broadcasted_iota(jnp.int32, (SSM_STATE, SLAB_STATE), 1) % SSM_STATE
             == lax.broadcasted_iota(jnp.int32, (SSM_STATE, SLAB_STATE), 0)).astype(F32)
    tb_re = _dot_exact(bre[...], rep_h)
    tb_im = _dot_exact(bim[...], rep_h)
    tc_re = _dot_exact(cre_in[...], rep_p)
    tc_im = _dot_exact(cim_in[...], rep_p)
    mask_b = _group_mask((SLAB_STATE, LANES), SSM_STATE, SSM_GROUP_CH)
    mask_c = _group_mask((LANES, SLAB_STATE), SSM_GROUP_CH, SSM_STATE)

    for j in range(SLABS):
        lanes = slice(j * SLAB_STATE, (j + 1) * SLAB_STATE)
        are_o[:, lanes] = jnp.broadcast_to(a_re[j:j + 1, :], (SUBLANES, SLAB_STATE))
        aim_o[:, lanes] = jnp.broadcast_to(a_im[j:j + 1, :], (SUBLANES, SLAB_STATE))
        rows = slice(j * SLAB_STATE, (j + 1) * SLAB_STATE)
        m_re = jnp.where(mask_b, tb_re[rows], 0.0).T
        m_im = jnp.where(mask_b, tb_im[rows], 0.0).T
        c_re, c_im = k_re[j:j + 1, :], k_im[j:j + 1, :]
        bmat_o[j] = jnp.concatenate([c_re * m_re - c_im * m_im, c_re * m_im + c_im * m_re],
                                    axis=1).astype(BF16)
        rows = slice(j * LANES, (j + 1) * LANES)
        read = jnp.concatenate([jnp.where(mask_c, tc_re[rows], 0.0),
                                jnp.where(mask_c, -tc_im[rows], 0.0)], axis=1)
        cmat_o[j] = read.T.astype(BF16)

    tril = (lax.broadcasted_iota(jnp.int32, (CHUNK, CHUNK), 0)
            >= lax.broadcasted_iota(jnp.int32, (CHUNK, CHUNK), 1))
    spw = []
    for g in range(GMLP_GROUPS):
        w = wsp_in[g]
        wsp_o[g] = jnp.where(tril, w, 0.0).astype(BF16)
        spw.append(jnp.broadcast_to(w[0:1, 0:1], (1, GROUP_CH)))
    spw_o[...] = jnp.concatenate(spw, axis=1)
    sel = _group_mask((D_MODEL, GMLP_GROUPS), GROUP_CH, 1).astype(F32)
    bsp_o[...] = _dot_exact(sel, bsp_in[...]).T


def _prep(lam_re, lam_im, log_dt, b_re, b_im, c_re, c_im, w_spatial, b_spatial):
    g, p, h = b_re.shape
    slab = lambda a: a.reshape(SLABS, SLAB_STATE)
    ldt = jnp.broadcast_to(log_dt[:, None], (g, p))
    return pl.pallas_call(
        _prep_kernel,
        out_shape=(jax.ShapeDtypeStruct((SUBLANES, STATE_W), F32),
                   jax.ShapeDtypeStruct((SUBLANES, STATE_W), F32),
                   jax.ShapeDtypeStruct((SLABS, LANES, 2 * SLAB_STATE), BF16),
                   jax.ShapeDtypeStruct((SLABS, 2 * SLAB_STATE, LANES), BF16),
                   jax.ShapeDtypeStruct(w_spatial.shape, BF16),
                   jax.ShapeDtypeStruct((CHUNK, D_MODEL), F32),
                   jax.ShapeDtypeStruct((1, D_MODEL), F32)),
        name="prep",
    )(slab(lam_re), slab(lam_im), slab(ldt), b_re.reshape(g * p, h), b_im.reshape(g * p, h),
      c_re.reshape(g * h, p), c_im.reshape(g * h, p), w_spatial, b_spatial)


def _mixer_prompt_kernel(x_ref, gmix, w_in, lng, lnb, wsp, bsp, are, aim, bmat, cmat, dsk,
                         w_a, w_b, w_out,
                         x1_ref, hre_o, him_o,
                         s_tm, ys_tm, ya_s, hn_s, bu, hbuf, st_re, st_im):
    i = pl.program_id(0)
    nb = x_ref.shape[0]
    rows = nb * CHUNK
    per_block = ROW_BLOCK // CHUNK

    @pl.when(i == 0)
    def _():
        st_re[...] = jnp.zeros_like(st_re)
        st_im[...] = jnp.zeros_like(st_im)

    def project(blk):
        r = slice(blk * ROW_BLOCK, (blk + 1) * ROW_BLOCK)
        x = jnp.concatenate([x_ref[blk * per_block + k] for k in range(per_block)], axis=0)
        hn = _rmsnorm(x, gmix[...]).astype(BF16)
        hn_s[r, :] = hn
        zu = _dot(hn, w_in[:, OFF_U:OFF_U + D_MODEL])
        zv = _dot(hn, w_in[:, OFF_V:OFF_V + D_MODEL])
        s = _dot(hn, w_in[:, OFF_S:OFF_S + SSM_WIDTH])
        for k in range(per_block):
            b = blk * per_block + k
            for j in range(SLABS):
                s_tm[j, pl.ds(b, CHUNK, stride=nb), :] = s[k * CHUNK:(k + 1) * CHUNK,
                                                          j * LANES:(j + 1) * LANES]
        return zu, zv

    def gate(blk, zu, zv):
        u = jax.nn.gelu(zu)
        vb = _layernorm(jax.nn.gelu(zv), lng[...], lnb[...]).astype(BF16)
        mixed = []
        for k in range(per_block):
            cols = [_dot(wsp[g], vb[k * CHUNK:(k + 1) * CHUNK, g * GROUP_CH:(g + 1) * GROUP_CH])
                    for g in range(GMLP_GROUPS)]
            mixed.append(jnp.concatenate(cols, axis=1) + bsp[...])
        gated = (u * jnp.concatenate(mixed, axis=0)).astype(BF16)
        ya_s[blk * ROW_BLOCK:(blk + 1) * ROW_BLOCK, :] = _dot(gated, w_a[...])

    n_blk = rows // ROW_BLOCK
    z = project(0)
    for blk in range(n_blk):
        z_next = project(blk + 1) if blk + 1 < n_blk else None
        gate(blk, *z)
        z = z_next

    sub_rows = SCAN_STEPS * nb

    def project_in(q):
        for j in range(SLABS):
            s_blk = s_tm[j, q * sub_rows:(q + 1) * sub_rows, :]
            bu[q % 2, j] = _dot(s_blk.astype(BF16), bmat[j])

    def recur(q):
        for j in range(SLABS):
            lanes = slice(j * SLAB_STATE, (j + 1) * SLAB_STATE)
            a_re, a_im = are[:, lanes], aim[:, lanes]
            h_re, h_im = st_re[:, lanes], st_im[:, lanes]
            for t in range(0, SCAN_STEPS, 2):
                pair = []
                for r in (slice(t * nb, (t + 1) * nb), slice((t + 1) * nb, (t + 2) * nb)):
                    n_re = a_re * h_re - a_im * h_im + bu[q % 2, j, r, 0:SLAB_STATE]
                    n_im = a_re * h_im + a_im * h_re + bu[q % 2, j, r, SLAB_STATE:2 * SLAB_STATE]
                    h_re, h_im = n_re, n_im
                    pair.append(jnp.concatenate([n_re, n_im], axis=1))
                hbuf[q % 2, j, t * nb:(t + 2) * nb, :] = jnp.concatenate(pair, axis=0).astype(BF16)
            st_re[:, lanes] = h_re
            st_im[:, lanes] = h_im

    def project_out(q):
        for j in range(SLABS):
            r = slice(q * sub_rows, (q + 1) * sub_rows)
            d_j = dsk[:, j * LANES:(j + 1) * LANES]
            ys_tm[j, r, :] = _dot(hbuf[q % 2, j], cmat[j]) + d_j * s_tm[j, r, :]

    n_sub = rows // sub_rows
    project_in(0)
    for q in range(n_sub):
        if q + 1 < n_sub:
            project_in(q + 1)
        recur(q)
        if q >= 1:
            project_out(q - 1)
    project_out(n_sub - 1)

    def phase3(blk, carry):
        r0 = pl.multiple_of(blk * ROW_BLOCK, ROW_BLOCK)
        ys = jnp.concatenate(
            [jnp.concatenate([ys_tm[j, pl.ds(blk * per_block + k, CHUNK, stride=nb), :]
                              for j in range(SLABS)], axis=1)
             for k in range(per_block)], axis=0)
        pb = _dot(ys.astype(BF16), w_b[...])
        yb = pb[:, :D_MODEL] * jax.nn.sigmoid(pb[:, D_MODEL:])
        hn = hn_s[pl.ds(r0, ROW_BLOCK), :]
        ga = jax.nn.sigmoid(_dot(hn, w_in[:, OFF_GA:OFF_GA + D_MODEL]))
        gb = jax.nn.sigmoid(_dot(hn, w_in[:, OFF_GB:OFF_GB + D_MODEL]))
        merged = (ga * ya_s[pl.ds(r0, ROW_BLOCK), :] + gb * yb).astype(BF16)
        y = _dot(merged, w_out[...])
        for k in range(per_block):
            b = blk * per_block + k
            x1_ref[b] = x_ref[b] + y[k * CHUNK:(k + 1) * CHUNK]
        return carry

    lax.fori_loop(0, rows // ROW_BLOCK, phase3, 0, unroll=2)

    @pl.when(i == pl.num_programs(0) - 1)
    def _():
        hre_o[...] = st_re[...]
        him_o[...] = st_im[...]


def _mixer_prompt(x, gmix, w_in, lng, lnb, wsp, bsp, are, aim, bmat, cmat, dsk, w_a, w_b, w_out):
    nb, seq, d = x.shape
    rows = nb * CHUNK
    consts = (gmix, w_in, lng, lnb, wsp, bsp, are, aim, bmat, cmat, dsk, w_a, w_b, w_out)
    x_spec = pl.BlockSpec((nb, CHUNK, d), lambda i: (0, i, 0))
    st_spec = pl.BlockSpec((nb, STATE_W), lambda i: (0, 0))
    return pl.pallas_call(
        _mixer_prompt_kernel,
        grid=(seq // CHUNK,),
        in_specs=[x_spec] + [_const_spec(c.shape) for c in consts],
        out_specs=(x_spec, st_spec, st_spec),
        out_shape=(jax.ShapeDtypeStruct(x.shape, F32),
                   jax.ShapeDtypeStruct((nb, STATE_W), F32),
                   jax.ShapeDtypeStruct((nb, STATE_W), F32)),
        scratch_shapes=[
            pltpu.VMEM((SLABS, rows, LANES), F32),
            pltpu.VMEM((SLABS, rows, LANES), F32),
            pltpu.VMEM((rows, d), F32),
            pltpu.VMEM((rows, d), BF16),
            pltpu.VMEM((2, SLABS, SCAN_STEPS * nb, 2 * SLAB_STATE), F32),
            pltpu.VMEM((2, SLABS, SCAN_STEPS * nb, 2 * SLAB_STATE), BF16),
            pltpu.VMEM((nb, STATE_W), F32),
            pltpu.VMEM((nb, STATE_W), F32),
        ],
        compiler_params=pltpu.CompilerParams(dimension_semantics=("arbitrary",),
                                             vmem_limit_bytes=VMEM_LIMIT),
        name="mixer_prompt",
    )(x, *consts)


def _mixer_sample_kernel(x_ref, h0re, h0im, gmix, w_in, lng, lnb, spw, bsp, are, aim, bmat, cmat,
                         dsk, w_a, w_b, w_out,
                         x1_ref, hre_o, him_o, v_o):
    x = x_ref[...]
    hn = _rmsnorm(x, gmix[...]).astype(BF16)
    u = jax.nn.gelu(_dot(hn, w_in[:, OFF_U:OFF_U + D_MODEL]))
    v = _layernorm(jax.nn.gelu(_dot(hn, w_in[:, OFF_V:OFF_V + D_MODEL])), lng[...], lnb[...])
    v_o[...] = v
    mixed = v * spw[...] + bsp[0:1, :]
    ya = _dot((u * mixed).astype(BF16), w_a[...])
    s = _dot(hn, w_in[:, OFF_S:OFF_S + SSM_WIDTH])
    ys = []
    for j in range(SLABS):
        lanes = slice(j * SLAB_STATE, (j + 1) * SLAB_STATE)
        s_j = s[:, j * LANES:(j + 1) * LANES]
        bu = _dot(s_j.astype(BF16), bmat[j])
        a_re, a_im = are[0:1, lanes], aim[0:1, lanes]
        p_re, p_im = h0re[:, lanes], h0im[:, lanes]
        n_re = a_re * p_re - a_im * p_im + bu[:, 0:SLAB_STATE]
        n_im = a_re * p_im + a_im * p_re + bu[:, SLAB_STATE:]
        hre_o[:, lanes] = n_re
        him_o[:, lanes] = n_im
        h = jnp.concatenate([n_re, n_im], axis=1).astype(BF16)
        ys.append(_dot(h, cmat[j]) + dsk[:, j * LANES:(j + 1) * LANES] * s_j)
    pb = _dot(jnp.concatenate(ys, axis=1).astype(BF16), w_b[...])
    yb = pb[:, :D_MODEL] * jax.nn.sigmoid(pb[:, D_MODEL:])
    ga = jax.nn.sigmoid(_dot(hn, w_in[:, OFF_GA:OFF_GA + D_MODEL]))
    gb = jax.nn.sigmoid(_dot(hn, w_in[:, OFF_GB:OFF_GB + D_MODEL]))
    merged = (ga * ya + gb * yb).astype(BF16)
    x1_ref[...] = x + _dot(merged, w_out[...])


def _mixer_sample(x, h0re, h0im, gmix, w_in, lng, lnb, spw, bsp, are, aim, bmat, cmat, dsk,
                  w_a, w_b, w_out):
    n, d = x.shape
    return pl.pallas_call(
        _mixer_sample_kernel,
        out_shape=(jax.ShapeDtypeStruct((n, d), F32),
                   jax.ShapeDtypeStruct((n, STATE_W), F32),
                   jax.ShapeDtypeStruct((n, STATE_W), F32),
                   jax.ShapeDtypeStruct((n, d), F32)),
        compiler_params=pltpu.CompilerParams(vmem_limit_bytes=VMEM_LIMIT),
        name="mixer_sample",
    )(x, h0re, h0im, gmix, w_in, lng, lnb, spw, bsp, are, aim, bmat, cmat, dsk, w_a, w_b, w_out)


def _ffn_kernel(x_ref, gffn, w_gate, w_up, w_down, gfin, y_ref):
    x = x_ref[...]
    h = _rmsnorm(x, gffn[...]).astype(BF16)
    act = (jax.nn.silu(_dot(h, w_gate[...])) * _dot(h, w_up[...])).astype(BF16)
    x2 = x + _dot(act, w_down[...])
    y_ref[...] = _rmsnorm(x2, gfin[...])


def _ffn(x, gffn, w_gate, w_up, w_down, gfin, block_rows):
    n, d = x.shape
    consts = (gffn, w_gate, w_up, w_down, gfin)
    row_spec = pl.BlockSpec((block_rows, d), lambda i: (i, 0))
    return pl.pallas_call(
        _ffn_kernel,
        grid=(n // block_rows,),
        in_specs=[row_spec] + [_const_spec(c.shape) for c in consts],
        out_specs=row_spec,
        out_shape=jax.ShapeDtypeStruct((n, d), F32),
        compiler_params=pltpu.CompilerParams(dimension_semantics=("arbitrary",),
                                             vmem_limit_bytes=VMEM_LIMIT),
        name="ffn",
    )(x, *consts)


def kernel(x_prompt, x_sample, state_ssm_re, state_ssm_im, norm_mix_g, w_in, ln_v_g, ln_v_b,
           w_spatial, b_spatial, ssm_lam_re, ssm_lam_im, ssm_log_dt, ssm_b_re, ssm_b_im,
           ssm_c_re, ssm_c_im, ssm_d, w_branch_a, w_branch_b, w_out, norm_ffn_g,
           w_gate_ffn, w_up_ffn, w_down_ffn, norm_final_g):
    depth = w_in.shape[0]
    assert depth == 1
    nb, seq, d = x_prompt.shape
    ns = x_sample.shape[0]
    row = lambda a: a.reshape(1, -1)

    xp = x_prompt
    xs = x_sample.reshape(ns, d)
    gfin = row(norm_final_g)
    p_re, p_im, s_re, s_im, s_v = [], [], [], [], []
    for l in range(depth):
        are, aim, bmat, cmat, wsp, bsp, spw = _prep(
            ssm_lam_re[l], ssm_lam_im[l], ssm_log_dt[l], ssm_b_re[l], ssm_b_im[l],
            ssm_c_re[l], ssm_c_im[l], w_spatial[l], b_spatial[l])
        dsk = row(ssm_d[l])
        shared = dict(gmix=row(norm_mix_g[l]), w_in=w_in[l].astype(BF16), lng=row(ln_v_g[l]),
                      lnb=row(ln_v_b[l]))
        tail = dict(are=are, aim=aim, bmat=bmat, cmat=cmat, dsk=dsk,
                    w_a=w_branch_a[l].astype(BF16), w_b=w_branch_b[l].astype(BF16),
                    w_out=w_out[l].astype(BF16))
        ffn_w = (row(norm_ffn_g[l]), w_gate_ffn[l].astype(BF16), w_up_ffn[l].astype(BF16),
                 w_down_ffn[l].astype(BF16))

        xp, hre, him = _mixer_prompt(xp, **shared, wsp=wsp, bsp=bsp, **tail)
        p_re.append(hre.reshape(nb, SSM_GROUPS, SSM_STATE))
        p_im.append(him.reshape(nb, SSM_GROUPS, SSM_STATE))

        xs, hre, him, v_rows = _mixer_sample(
            xs, state_ssm_re[l].reshape(ns, STATE_W), state_ssm_im[l].reshape(ns, STATE_W),
            **shared, spw=spw, bsp=bsp, **tail)
        s_re.append(hre.reshape(ns, SSM_GROUPS, SSM_STATE))
        s_im.append(him.reshape(ns, SSM_GROUPS, SSM_STATE))
        s_v.append(v_rows.reshape(ns, 1, d))

        xp = _ffn(xp.reshape(nb * seq, d), *ffn_w, gfin, 512).reshape(nb, seq, d)
        xs = _ffn(xs, *ffn_w, gfin, ns)

    return (xp, xs.reshape(ns, 1, d), jnp.stack(p_re), jnp.stack(p_im), jnp.stack(s_re),
            jnp.stack(s_im), jnp.stack(s_v))
```

```python
import functools

import jax
import jax.numpy as jnp
from jax import lax
from jax.experimental import pallas as pl
from jax.experimental.pallas import tpu as pltpu

F32 = jnp.float32
BF16 = jnp.bfloat16

D_MODEL = 1024
CHUNK = 128
GMLP_GROUPS = 8
GROUP_CH = D_MODEL // GMLP_GROUPS
SSM_WIDTH = 512
SSM_GROUPS = 32
SSM_STATE = 64
SSM_GROUP_CH = 16
LANES = 128
SUBLANES = 8
SLABS = SSM_WIDTH // LANES
SLAB_GROUPS = LANES // SSM_GROUP_CH
SLAB_STATE = SLAB_GROUPS * SSM_STATE
STATE_W = SSM_GROUPS * SSM_STATE
IN_WIDTH = 2 * D_MODEL + SSM_WIDTH + 2 * D_MODEL
D_FF = 2816
EPS = 1e-6

OFF_U, OFF_V, OFF_S = 0, D_MODEL, 2 * D_MODEL
OFF_GA = 2 * D_MODEL + SSM_WIDTH
OFF_GB = OFF_GA + D_MODEL

ROW_BLOCK = 256
SCAN_STEPS = 16
VMEM_LIMIT = 56 * 1024 * 1024


def _dot(a, b):
    return jnp.dot(a, b, preferred_element_type=F32)


def _rmsnorm(x, g):
    return x * lax.rsqrt(jnp.mean(x * x, axis=-1, keepdims=True) + EPS) * g


def _layernorm(x, g, b):
    mu = jnp.mean(x, axis=-1, keepdims=True)
    xc = x - mu
    var = jnp.mean(xc * xc, axis=-1, keepdims=True)
    return xc * lax.rsqrt(var + EPS) * g + b


def _const_spec(shape):
    zeros = (0,) * len(shape)
    return pl.BlockSpec(shape, lambda *_: zeros, pipeline_mode=pl.Buffered(1))


def _dot_exact(a, b):
    return jnp.dot(a, b, preferred_element_type=F32, precision=lax.Precision.HIGHEST)


def _group_mask(shape, row_div, lane_div):
    rows = lax.broadcasted_iota(jnp.int32, shape, 0) // row_div
    lanes = lax.broadcasted_iota(jnp.int32, shape, 1) // lane_div
    return rows == lanes


def _prep_kernel(lre, lim, ldt, bre, bim, cre_in, cim_in, wsp_in, bsp_in,
                 are_o, aim_o, bmat_o, cmat_o, wsp_o, bsp_o, spw_o):
    def slab_rows(row_of_group):
        return jnp.concatenate(
            [jnp.concatenate([row_of_group(j * SLAB_GROUPS + k) for k in range(SLAB_GROUPS)], axis=1)
             for j in range(SLABS)], axis=0)

    lr = slab_rows(lambda g: lre[g:g + 1, :])
    li = slab_rows(lambda g: lim[g:g + 1, :])
    dt = jnp.exp(slab_rows(lambda g: jnp.broadcast_to(ldt[0:1, g:g + 1], (1, SSM_STATE))))
    mag = jnp.exp(lr * dt)
    a_re, a_im = mag * jnp.cos(li * dt), mag * jnp.sin(li * dt)
    n_re, n_im = a_re - 1.0, a_im
    den = lr * lr + li * li
    k_re = (n_re * lr + n_im * li) / den
    k_im = (n_im * lr - n_re * li) / den

    rep_h = (lax.broadcasted_iota(jnp.int32, (SSM_GROUP_CH, LANES), 1) % SSM_GROUP_CH
             == lax.broadcasted_iota(jnp.int32, (SSM_GROUP_CH, LANES), 0)).astype(F32)
    rep_p = (lax.broadcasted_iota(jnp.int32, (SSM_STATE, SLAB_STATE), 1) % SSM_STATE
             == lax.broadcasted_iota(jnp.int32, (SSM_STATE, SLAB_STATE), 0)).astype(F32)
    tb_re = _dot_exact(bre[...], rep_h)
    tb_im = _dot_exact(bim[...], rep_h)
    tc_re = _dot_exact(cre_in[...], rep_p)
    tc_im = _dot_exact(cim_in[...], rep_p)
    mask_b = _group_mask((SLAB_STATE, LANES), SSM_STATE, SSM_GROUP_CH)
    mask_c = _group_mask((LANES, SLAB_STATE), SSM_GROUP_CH, SSM_STATE)

    for j in range(SLABS):
        lanes = slice(j * SLAB_STATE, (j + 1) * SLAB_STATE)
        are_o[:, lanes] = jnp.broadcast_to(a_re[j:j + 1, :], (SUBLANES, SLAB_STATE))
        aim_o[:, lanes] = jnp.broadcast_to(a_im[j:j + 1, :], (SUBLANES, SLAB_STATE))
        rows = slice(j * SLAB_STATE, (j + 1) * SLAB_STATE)
        m_re = jnp.where(mask_b, tb_re[rows], 0.0).T
        m_im = jnp.where(mask_b, tb_im[rows], 0.0).T
        c_re, c_im = k_re[j:j + 1, :], k_im[j:j + 1, :]
        bmat_o[j] = jnp.concatenate([c_re * m_re - c_im * m_im, c_re * m_im + c_im * m_re],
                                    axis=1).astype(BF16)
        rows = slice(j * LANES, (j + 1) * LANES)
        read = jnp.concatenate([jnp.where(mask_c, tc_re[rows], 0.0),
                                jnp.where(mask_c, -tc_im[rows], 0.0)], axis=1)
        cmat_o[j] = read.T.astype(BF16)

    tril = (lax.broadcasted_iota(jnp.int32, (CHUNK, CHUNK), 0)
            >= lax.broadcasted_iota(jnp.int32, (CHUNK, CHUNK), 1))
    spw = []
    for g in range(GMLP_GROUPS):
        w = wsp_in[g]
        wsp_o[g] = jnp.where(tril, w, 0.0).astype(BF16)
        spw.append(jnp.broadcast_to(w[0:1, 0:1], (1, GROUP_CH)))
    spw_o[...] = jnp.concatenate(spw, axis=1)
    sel = _group_mask((D_MODEL, GMLP_GROUPS), GROUP_CH, 1).astype(F32)
    bsp_o[...] = _dot_exact(sel, bsp_in[...]).T


def _prep(lam_re, lam_im, log_dt, b_re, b_im, c_re, c_im, w_spatial, b_spatial):
    g, p, h = b_re.shape
    return pl.pallas_call(
        _prep_kernel,
        out_shape=(jax.ShapeDtypeStruct((SUBLANES, STATE_W), F32),
                   jax.ShapeDtypeStruct((SUBLANES, STATE_W), F32),
                   jax.ShapeDtypeStruct((SLABS, LANES, 2 * SLAB_STATE), BF16),
                   jax.ShapeDtypeStruct((SLABS, 2 * SLAB_STATE, LANES), BF16),
                   jax.ShapeDtypeStruct(w_spatial.shape, BF16),
                   jax.ShapeDtypeStruct((CHUNK, D_MODEL), F32),
                   jax.ShapeDtypeStruct((1, D_MODEL), F32)),
        name="prep",
    )(lam_re, lam_im, log_dt.reshape(1, g), b_re.reshape(g * p, h), b_im.reshape(g * p, h),
      c_re.reshape(g * h, p), c_im.reshape(g * h, p), w_spatial, b_spatial)


def _mixer_prompt_kernel(x_ref, gmix, w_in, lng, lnb, wsp, bsp, are, aim, bmat, cmat, dsk,
                         w_a, w_b, w_out,
                         x1_ref, hre_o, him_o,
                         s_tm, ys_tm, ya_s, hn_s, bu, hbuf, st_re, st_im):
    i = pl.program_id(0)
    nb = x_ref.shape[0]
    rows = nb * CHUNK
    per_block = ROW_BLOCK // CHUNK

    @pl.when(i == 0)
    def _():
        st_re[...] = jnp.zeros_like(st_re)
        st_im[...] = jnp.zeros_like(st_im)

    def project(blk):
        r = slice(blk * ROW_BLOCK, (blk + 1) * ROW_BLOCK)
        x = jnp.concatenate([x_ref[blk * per_block + k] for k in range(per_block)], axis=0)
        hn = _rmsnorm(x, gmix[...]).astype(BF16)
        hn_s[r, :] = hn
        zu = _dot(hn, w_in[:, OFF_U:OFF_U + D_MODEL])
        zv = _dot(hn, w_in[:, OFF_V:OFF_V + D_MODEL])
        s = _dot(hn, w_in[:, OFF_S:OFF_S + SSM_WIDTH])
        for k in range(per_block):
            b = blk * per_block + k
            for j in range(SLABS):
                s_tm[j, pl.ds(b, CHUNK, stride=nb), :] = s[k * CHUNK:(k + 1) * CHUNK,
                                                          j * LANES:(j + 1) * LANES]
        return zu, zv

    def gate(blk, zu, zv):
        u = jax.nn.gelu(zu)
        vb = _layernorm(jax.nn.gelu(zv), lng[...], lnb[...]).astype(BF16)
        mixed = []
        for k in range(per_block):
            cols = [_dot(wsp[g], vb[k * CHUNK:(k + 1) * CHUNK, g * GROUP_CH:(g + 1) * GROUP_CH])
                    for g in range(GMLP_GROUPS)]
            mixed.append(jnp.concatenate(cols, axis=1) + bsp[...])
        gated = (u * jnp.concatenate(mixed, axis=0)).astype(BF16)
        ya_s[blk * ROW_BLOCK:(blk + 1) * ROW_BLOCK, :] = _dot(gated, w_a[...])

    n_blk = rows // ROW_BLOCK
    z = project(0)
    for blk in range(n_blk):
        z_next = project(blk + 1) if blk + 1 < n_blk else None
        gate(blk, *z)
        z = z_next

    sub_rows = SCAN_STEPS * nb

    def project_in(q):
        for j in range(SLABS):
            s_blk = s_tm[j, q * sub_rows:(q + 1) * sub_rows, :]
            bu[q % 2, j] = _dot(s_blk.astype(BF16), bmat[j])

    def recur(q):
        for j in range(SLABS):
            lanes = slice(j * SLAB_STATE, (j + 1) * SLAB_STATE)
            a_re, a_im = are[:, lanes], aim[:, lanes]
            h_re, h_im = st_re[:, lanes], st_im[:, lanes]
            for t in range(0, SCAN_STEPS, 2):
                pair = []
                for r in (slice(t * nb, (t + 1) * nb), slice((t + 1) * nb, (t + 2) * nb)):
                    n_re = a_re * h_re - a_im * h_im + bu[q % 2, j, r, 0:SLAB_STATE]
                    n_im = a_re * h_im + a_im * h_re + bu[q % 2, j, r, SLAB_STATE:2 * SLAB_STATE]
                    h_re, h_im = n_re, n_im
                    pair.append(jnp.concatenate([n_re, n_im], axis=1))
                hbuf[q % 2, j, t * nb:(t + 2) * nb, :] = jnp.concatenate(pair, axis=0).astype(BF16)
            st_re[:, lanes] = h_re
            st_im[:, lanes] = h_im

    def project_out(q):
        for j in range(SLABS):
            r = slice(q * sub_rows, (q + 1) * sub_rows)
            d_j = dsk[:, j * LANES:(j + 1) * LANES]
            ys_tm[j, r, :] = _dot(hbuf[q % 2, j], cmat[j]) + d_j * s_tm[j, r, :]

    n_sub = rows // sub_rows
    project_in(0)
    for q in range(n_sub):
        if q + 1 < n_sub:
            project_in(q + 1)
        recur(q)
        if q >= 1:
            project_out(q - 1)
    project_out(n_sub - 1)

    def phase3(blk, carry):
        r0 = pl.multiple_of(blk * ROW_BLOCK, ROW_BLOCK)
        ys = jnp.concatenate(
            [jnp.concatenate([ys_tm[j, pl.ds(blk * per_block + k, CHUNK, stride=nb), :]
                              for j in range(SLABS)], axis=1)
             for k in range(per_block)], axis=0)
        pb = _dot(ys.astype(BF16), w_b[...])
        yb = pb[:, :D_MODEL] * jax.nn.sigmoid(pb[:, D_MODEL:])
        hn = hn_s[pl.ds(r0, ROW_BLOCK), :]
        ga = jax.nn.sigmoid(_dot(hn, w_in[:, OFF_GA:OFF_GA + D_MODEL]))
        gb = jax.nn.sigmoid(_dot(hn, w_in[:, OFF_GB:OFF_GB + D_MODEL]))
        merged = (ga * ya_s[pl.ds(r0, ROW_BLOCK), :] + gb * yb).astype(BF16)
        y = _dot(merged, w_out[...])
        for k in range(per_block):
            b = blk * per_block + k
            x1_ref[b] = x_ref[b] + y[k * CHUNK:(k + 1) * CHUNK]
        return carry

    lax.fori_loop(0, rows // ROW_BLOCK, phase3, 0, unroll=2)

    @pl.when(i == pl.num_programs(0) - 1)
    def _():
        hre_o[...] = st_re[...]
        him_o[...] = st_im[...]


def _mixer_prompt(x, gmix, w_in, lng, lnb, wsp, bsp, are, aim, bmat, cmat, dsk, w_a, w_b, w_out):
    nb, seq, d = x.shape
    rows = nb * CHUNK
    consts = (gmix, w_in, lng, lnb, wsp, bsp, are, aim, bmat, cmat, dsk, w_a, w_b, w_out)
    x_spec = pl.BlockSpec((nb, CHUNK, d), lambda i: (0, i, 0))
    st_spec = pl.BlockSpec((nb, STATE_W), lambda i: (0, 0))
    return pl.pallas_call(
        _mixer_prompt_kernel,
        grid=(seq // CHUNK,),
        in_specs=[x_spec] + [_const_spec(c.shape) for c in consts],
        out_specs=(x_spec, st_spec, st_spec),
        out_shape=(jax.ShapeDtypeStruct(x.shape, F32),
                   jax.ShapeDtypeStruct((nb, STATE_W), F32),
                   jax.ShapeDtypeStruct((nb, STATE_W), F32)),
        scratch_shapes=[
            pltpu.VMEM((SLABS, rows, LANES), F32),
            pltpu.VMEM((SLABS, rows, LANES), F32),
            pltpu.VMEM((rows, d), F32),
            pltpu.VMEM((rows, d), BF16),
            pltpu.VMEM((2, SLABS, SCAN_STEPS * nb, 2 * SLAB_STATE), F32),
            pltpu.VMEM((2, SLABS, SCAN_STEPS * nb, 2 * SLAB_STATE), BF16),
            pltpu.VMEM((nb, STATE_W), F32),
            pltpu.VMEM((nb, STATE_W), F32),
        ],
        compiler_params=pltpu.CompilerParams(dimension_semantics=("arbitrary",),
                                             vmem_limit_bytes=VMEM_LIMIT),
        name="mixer_prompt",
    )(x, *consts)


def _mixer_sample_kernel(x_ref, h0re, h0im, gmix, w_in, lng, lnb, spw, bsp, are, aim, bmat, cmat,
                         dsk, w_a, w_b, w_out,
                         x1_ref, hre_o, him_o, v_o):
    x = x_ref[:, 0, :]
    hn = _rmsnorm(x, gmix[...]).astype(BF16)
    u = jax.nn.gelu(_dot(hn, w_in[:, OFF_U:OFF_U + D_MODEL]))
    v = _layernorm(jax.nn.gelu(_dot(hn, w_in[:, OFF_V:OFF_V + D_MODEL])), lng[...], lnb[...])
    v_o[:, 0, :] = v
    mixed = v * spw[...] + bsp[0:1, :]
    ya = _dot((u * mixed).astype(BF16), w_a[...])
    s = _dot(hn, w_in[:, OFF_S:OFF_S + SSM_WIDTH])
    ys = []
    for j in range(SLABS):
        lanes = slice(j * SLAB_STATE, (j + 1) * SLAB_STATE)
        s_j = s[:, j * LANES:(j + 1) * LANES]
        bu = _dot(s_j.astype(BF16), bmat[j])
        a_re, a_im = are[0:1, lanes], aim[0:1, lanes]
        p_re, p_im = h0re[:, lanes], h0im[:, lanes]
        n_re = a_re * p_re - a_im * p_im + bu[:, 0:SLAB_STATE]
        n_im = a_re * p_im + a_im * p_re + bu[:, SLAB_STATE:]
        hre_o[:, lanes] = n_re
        him_o[:, lanes] = n_im
        h = jnp.concatenate([n_re, n_im], axis=1).astype(BF16)
        ys.append(_dot(h, cmat[j]) + dsk[:, j * LANES:(j + 1) * LANES] * s_j)
    pb = _dot(jnp.concatenate(ys, axis=1).astype(BF16), w_b[...])
    yb = pb[:, :D_MODEL] * jax.nn.sigmoid(pb[:, D_MODEL:])
    ga = jax.nn.sigmoid(_dot(hn, w_in[:, OFF_GA:OFF_GA + D_MODEL]))
    gb = jax.nn.sigmoid(_dot(hn, w_in[:, OFF_GB:OFF_GB + D_MODEL]))
    merged = (ga * ya + gb * yb).astype(BF16)
    x1_ref[...] = x + _dot(merged, w_out[...])


def _mixer_sample(x, h0re, h0im, gmix, w_in, lng, lnb, spw, bsp, are, aim, bmat, cmat, dsk,
                  w_a, w_b, w_out):
    n, _, d = x.shape
    return pl.pallas_call(
        _mixer_sample_kernel,
        out_shape=(jax.ShapeDtypeStruct((n, d), F32),
                   jax.ShapeDtypeStruct((n, STATE_W), F32),
                   jax.ShapeDtypeStruct((n, STATE_W), F32),
                   jax.ShapeDtypeStruct((n, 1, d), F32)),
        compiler_params=pltpu.CompilerParams(vmem_limit_bytes=VMEM_LIMIT),
        name="mixer_sample",
    )(x, h0re, h0im, gmix, w_in, lng, lnb, spw, bsp, are, aim, bmat, cmat, dsk, w_a, w_b, w_out)


def _ffn_rows(x, gffn, w_gate, w_up, w_down, gfin):
    h = _rmsnorm(x, gffn[...]).astype(BF16)
    act = (jax.nn.silu(_dot(h, w_gate[...])) * _dot(h, w_up[...])).astype(BF16)
    x2 = x + _dot(act, w_down[...])
    return _rmsnorm(x2, gfin[...])


def _ffn_kernel(x_ref, xs_ref, gffn, w_gate, w_up, w_down, gfin, y_ref, ys_ref):
    params = (gffn, w_gate, w_up, w_down, gfin)
    y_ref[...] = _ffn_rows(x_ref[...], *params)

    @pl.when(pl.program_id(0) == pl.num_programs(0) - 1)
    def _():
        ys_ref[:, 0, :] = _ffn_rows(xs_ref[...], *params)


def _ffn(x, xs, gffn, w_gate, w_up, w_down, gfin, block_rows):
    n, d = x.shape
    ns = xs.shape[0]
    consts = (gffn, w_gate, w_up, w_down, gfin)
    row_spec = pl.BlockSpec((block_rows, d), lambda i: (i, 0))
    return pl.pallas_call(
        _ffn_kernel,
        grid=(n // block_rows,),
        in_specs=[row_spec, _const_spec(xs.shape)] + [_const_spec(c.shape) for c in consts],
        out_specs=(row_spec, pl.BlockSpec((ns, 1, d), lambda i: (0, 0, 0))),
        out_shape=(jax.ShapeDtypeStruct((n, d), F32), jax.ShapeDtypeStruct((ns, 1, d), F32)),
        compiler_params=pltpu.CompilerParams(dimension_semantics=("arbitrary",),
                                             vmem_limit_bytes=VMEM_LIMIT),
        name="ffn",
    )(x, xs, *consts)


def kernel(x_prompt, x_sample, state_ssm_re, state_ssm_im, norm_mix_g, w_in, ln_v_g, ln_v_b,
           w_spatial, b_spatial, ssm_lam_re, ssm_lam_im, ssm_log_dt, ssm_b_re, ssm_b_im,
           ssm_c_re, ssm_c_im, ssm_d, w_branch_a, w_branch_b, w_out, norm_ffn_g,
           w_gate_ffn, w_up_ffn, w_down_ffn, norm_final_g):
    depth = w_in.shape[0]
    assert depth == 1
    nb, seq, d = x_prompt.shape
    ns = x_sample.shape[0]
    row = lambda a: a.reshape(1, -1)

    xp = x_prompt
    xs = x_sample
    gfin = row(norm_final_g)
    p_re, p_im, s_re, s_im, s_v = [], [], [], [], []
    for l in range(depth):
        are, aim, bmat, cmat, wsp, bsp, spw = _prep(
            ssm_lam_re[l], ssm_lam_im[l], ssm_log_dt[l], ssm_b_re[l], ssm_b_im[l],
            ssm_c_re[l], ssm_c_im[l], w_spatial[l], b_spatial[l])
        dsk = row(ssm_d[l])
        shared = dict(gmix=row(norm_mix_g[l]), w_in=w_in[l].astype(BF16), lng=row(ln_v_g[l]),
                      lnb=row(ln_v_b[l]))
        tail = dict(are=are, aim=aim, bmat=bmat, cmat=cmat, dsk=dsk,
                    w_a=w_branch_a[l].astype(BF16), w_b=w_branch_b[l].astype(BF16),
                    w_out=w_out[l].astype(BF16))
        ffn_w = (row(norm_ffn_g[l]), w_gate_ffn[l].astype(BF16), w_up_ffn[l].astype(BF16),
                 w_down_ffn[l].astype(BF16))

        xp, hre, him = _mixer_prompt(xp, **shared, wsp=wsp, bsp=bsp, **tail)
        p_re.append(hre.reshape(nb, SSM_GROUPS, SSM_STATE))
        p_im.append(him.reshape(nb, SSM_GROUPS, SSM_STATE))

        xs, hre, him, v_rows = _mixer_sample(
            xs, state_ssm_re[l].reshape(ns, STATE_W), state_ssm_im[l].reshape(ns, STATE_W),
            **shared, spw=spw, bsp=bsp, **tail)
        s_re.append(hre.reshape(ns, SSM_GROUPS, SSM_STATE))
        s_im.append(him.reshape(ns, SSM_GROUPS, SSM_STATE))
        s_v.append(v_rows)

        xp, xs = _ffn(xp.reshape(nb * seq, d), xs, *ffn_w, gfin, 512)
        xp = xp.reshape(nb, seq, d)

    return (xp, xs, jnp.stack(p_re), jnp.stack(p_im), jnp.stack(s_re),
            jnp.stack(s_im), jnp.stack(s_v))
```

```python
import functools

import jax
import jax.numpy as jnp
from jax import lax
from jax.experimental import pallas as pl
from jax.experimental.pallas import tpu as pltpu

F32 = jnp.float32
BF16 = jnp.bfloat16

D_MODEL = 1024
CHUNK = 128
GMLP_GROUPS = 8
GROUP_CH = D_MODEL // GMLP_GROUPS
SSM_WIDTH = 512
SSM_GROUPS = 32
SSM_STATE = 64
SSM_GROUP_CH = 16
LANES = 128
SUBLANES = 8
SLABS = SSM_WIDTH // LANES
SLAB_GROUPS = LANES // SSM_GROUP_CH
SLAB_STATE = SLAB_GROUPS * SSM_STATE
STATE_W = SSM_GROUPS * SSM_STATE
IN_WIDTH = 2 * D_MODEL + SSM_WIDTH + 2 * D_MODEL
D_FF = 2816
EPS = 1e-6

OFF_U, OFF_V, OFF_S = 0, D_MODEL, 2 * D_MODEL
OFF_GA = 2 * D_MODEL + SSM_WIDTH
OFF_GB = OFF_GA + D_MODEL

ROW_BLOCK = 256
SCAN_STEPS = 16
VMEM_LIMIT = 56 * 1024 * 1024


def _dot(a, b):
    return jnp.dot(a, b, preferred_element_type=F32)


def _rmsnorm(x, g):
    return x * lax.rsqrt(jnp.mean(x * x, axis=-1, keepdims=True) + EPS) * g


def _layernorm(x, g, b):
    mu = jnp.mean(x, axis=-1, keepdims=True)
    xc = x - mu
    var = jnp.mean(xc * xc, axis=-1, keepdims=True)
    return xc * lax.rsqrt(var + EPS) * g + b


def _const_spec(shape):
    zeros = (0,) * len(shape)
    return pl.BlockSpec(shape, lambda *_: zeros, pipeline_mode=pl.Buffered(1))


STAGE_SLOTS = 4
STAGE_ROWS, STAGE_COLS = 256, 1024


def _cast_pieces(src_hbm, dst_vmem):
    n_rows, n_cols = src_hbm.shape
    assert n_rows % STAGE_ROWS == 0 and n_cols % LANES == 0
    pieces = []
    for r in range(0, n_rows, STAGE_ROWS):
        for c in range(0, n_cols, STAGE_COLS):
            w = min(STAGE_COLS, n_cols - c)
            pieces.append((src_hbm.at[r:r + STAGE_ROWS, c:c + w],
                           dst_vmem.at[r:r + STAGE_ROWS, c:c + w], w))
    return pieces


def _load_cast(pieces, stage_view, sems):
    def copy(k):
        src, _, width = pieces[k]
        return pltpu.make_async_copy(src, stage_view(k % STAGE_SLOTS, width), sems.at[k % STAGE_SLOTS])

    for k in range(min(STAGE_SLOTS, len(pieces))):
        copy(k).start()
    for k, (_, dst, width) in enumerate(pieces):
        copy(k).wait()
        dst[...] = stage_view(k % STAGE_SLOTS, width)[...].astype(BF16)
        if k + STAGE_SLOTS < len(pieces):
            copy(k + STAGE_SLOTS).start()


def _dot_exact(a, b):
    return jnp.dot(a, b, preferred_element_type=F32, precision=lax.Precision.HIGHEST)


def _group_mask(shape, row_div, lane_div):
    rows = lax.broadcasted_iota(jnp.int32, shape, 0) // row_div
    lanes = lax.broadcasted_iota(jnp.int32, shape, 1) // lane_div
    return rows == lanes


def _prep_kernel(lre, lim, ldt, bre, bim, cre_in, cim_in, wsp_in, bsp_in,
                 are_o, aim_o, bmat_o, cmat_o, wsp_o, bsp_o, spw_o):
    def slab_rows(row_of_group):
        return jnp.concatenate(
            [jnp.concatenate([row_of_group(j * SLAB_GROUPS + k) for k in range(SLAB_GROUPS)], axis=1)
             for j in range(SLABS)], axis=0)

    lr = slab_rows(lambda g: lre[g:g + 1, :])
    li = slab_rows(lambda g: lim[g:g + 1, :])
    dt = jnp.exp(slab_rows(lambda g: jnp.broadcast_to(ldt[0:1, g:g + 1], (1, SSM_STATE))))
    mag = jnp.exp(lr * dt)
    a_re, a_im = mag * jnp.cos(li * dt), mag * jnp.sin(li * dt)
    n_re, n_im = a_re - 1.0, a_im
    den = lr * lr + li * li
    k_re = (n_re * lr + n_im * li) / den
    k_im = (n_im * lr - n_re * li) / den

    rep_h = (lax.broadcasted_iota(jnp.int32, (SSM_GROUP_CH, LANES), 1) % SSM_GROUP_CH
             == lax.broadcasted_iota(jnp.int32, (SSM_GROUP_CH, LANES), 0)).astype(F32)
    rep_p = (lax.broadcasted_iota(jnp.int32, (SSM_STATE, SLAB_STATE), 1) % SSM_STATE
             == lax.broadcasted_iota(jnp.int32, (SSM_STATE, SLAB_STATE), 0)).astype(F32)
    tb_re = _dot_exact(bre[...], rep_h)
    tb_im = _dot_exact(bim[...], rep_h)
    tc_re = _dot_exact(cre_in[...], rep_p)
    tc_im = _dot_exact(cim_in[...], rep_p)
    mask_b = _group_mask((SLAB_STATE, LANES), SSM_STATE, SSM_GROUP_CH)
    mask_c = _group_mask((LANES, SLAB_STATE), SSM_GROUP_CH, SSM_STATE)

    for j in range(SLABS):
        lanes = slice(j * SLAB_STATE, (j + 1) * SLAB_STATE)
        are_o[:, lanes] = jnp.broadcast_to(a_re[j:j + 1, :], (SUBLANES, SLAB_STATE))
        aim_o[:, lanes] = jnp.broadcast_to(a_im[j:j + 1, :], (SUBLANES, SLAB_STATE))
        rows = slice(j * SLAB_STATE, (j + 1) * SLAB_STATE)
        m_re = jnp.where(mask_b, tb_re[rows], 0.0).T
        m_im = jnp.where(mask_b, tb_im[rows], 0.0).T
        c_re, c_im = k_re[j:j + 1, :], k_im[j:j + 1, :]
        bmat_o[j] = jnp.concatenate([c_re * m_re - c_im * m_im, c_re * m_im + c_im * m_re],
                                    axis=1).astype(BF16)
        rows = slice(j * LANES, (j + 1) * LANES)
        read = jnp.concatenate([jnp.where(mask_c, tc_re[rows], 0.0),
                                jnp.where(mask_c, -tc_im[rows], 0.0)], axis=1)
        cmat_o[j] = read.T.astype(BF16)

    tril = (lax.broadcasted_iota(jnp.int32, (CHUNK, CHUNK), 0)
            >= lax.broadcasted_iota(jnp.int32, (CHUNK, CHUNK), 1))
    spw = []
    for g in range(GMLP_GROUPS):
        w = wsp_in[g]
        wsp_o[g] = jnp.where(tril, w, 0.0).astype(BF16)
        spw.append(jnp.broadcast_to(w[0:1, 0:1], (1, GROUP_CH)))
    spw_o[...] = jnp.concatenate(spw, axis=1)
    sel = _group_mask((D_MODEL, GMLP_GROUPS), GROUP_CH, 1).astype(F32)
    bsp_o[...] = _dot_exact(sel, bsp_in[...]).T


def _prep(lam_re, lam_im, log_dt, b_re, b_im, c_re, c_im, w_spatial, b_spatial):
    g, p, h = b_re.shape
    return pl.pallas_call(
        _prep_kernel,
        out_shape=(jax.ShapeDtypeStruct((SUBLANES, STATE_W), F32),
                   jax.ShapeDtypeStruct((SUBLANES, STATE_W), F32),
                   jax.ShapeDtypeStruct((SLABS, LANES, 2 * SLAB_STATE), BF16),
                   jax.ShapeDtypeStruct((SLABS, 2 * SLAB_STATE, LANES), BF16),
                   jax.ShapeDtypeStruct(w_spatial.shape, BF16),
                   jax.ShapeDtypeStruct((CHUNK, D_MODEL), F32),
                   jax.ShapeDtypeStruct((1, D_MODEL), F32)),
        name="prep",
    )(lam_re, lam_im, log_dt.reshape(1, g), b_re.reshape(g * p, h), b_im.reshape(g * p, h),
      c_re.reshape(g * h, p), c_im.reshape(g * h, p), w_spatial, b_spatial)


def _mixer_prompt_kernel(x_ref, gmix, w_in, lng, lnb, wsp, bsp, are, aim, bmat, cmat, dsk,
                         w_a, w_b, w_out,
                         x1_ref, hre_o, him_o,
                         s_tm, ys_tm, ya_s, hn_s, bu, hbuf, st_re, st_im):
    i = pl.program_id(0)
    nb = x_ref.shape[0]
    rows = nb * CHUNK
    per_block = ROW_BLOCK // CHUNK

    @pl.when(i == 0)
    def _():
        st_re[...] = jnp.zeros_like(st_re)
        st_im[...] = jnp.zeros_like(st_im)

    def project(blk):
        r = slice(blk * ROW_BLOCK, (blk + 1) * ROW_BLOCK)
        x = jnp.concatenate([x_ref[blk * per_block + k] for k in range(per_block)], axis=0)
        hn = _rmsnorm(x, gmix[...]).astype(BF16)
        hn_s[r, :] = hn
        zu = _dot(hn, w_in[:, OFF_U:OFF_U + D_MODEL])
        zv = _dot(hn, w_in[:, OFF_V:OFF_V + D_MODEL])
        s = _dot(hn, w_in[:, OFF_S:OFF_S + SSM_WIDTH])
        for k in range(per_block):
            b = blk * per_block + k
            for j in range(SLABS):
                s_tm[j, pl.ds(b, CHUNK, stride=nb), :] = s[k * CHUNK:(k + 1) * CHUNK,
                                                          j * LANES:(j + 1) * LANES]
        return zu, zv

    def gate(blk, zu, zv):
        u = jax.nn.gelu(zu)
        vb = _layernorm(jax.nn.gelu(zv), lng[...], lnb[...]).astype(BF16)
        mixed = []
        for k in range(per_block):
            cols = [_dot(wsp[g], vb[k * CHUNK:(k + 1) * CHUNK, g * GROUP_CH:(g + 1) * GROUP_CH])
                    for g in range(GMLP_GROUPS)]
            mixed.append(jnp.concatenate(cols, axis=1) + bsp[...])
        gated = (u * jnp.concatenate(mixed, axis=0)).astype(BF16)
        ya_s[blk * ROW_BLOCK:(blk + 1) * ROW_BLOCK, :] = _dot(gated, w_a[...])

    n_blk = rows // ROW_BLOCK
    z = project(0)
    for blk in range(n_blk):
        z_next = project(blk + 1) if blk + 1 < n_blk else None
        gate(blk, *z)
        z = z_next

    sub_rows = SCAN_STEPS * nb

    def project_in(q):
        for j in range(SLABS):
            s_blk = s_tm[j, q * sub_rows:(q + 1) * sub_rows, :]
            bu[q % 2, j] = _dot(s_blk.astype(BF16), bmat[j])

    def recur(q):
        for j in range(SLABS):
            lanes = slice(j * SLAB_STATE, (j + 1) * SLAB_STATE)
            a_re, a_im = are[:, lanes], aim[:, lanes]
            h_re, h_im = st_re[:, lanes], st_im[:, lanes]
            for t in range(0, SCAN_STEPS, 2):
                pair = []
                for r in (slice(t * nb, (t + 1) * nb), slice((t + 1) * nb, (t + 2) * nb)):
                    n_re = a_re * h_re - a_im * h_im + bu[q % 2, j, r, 0:SLAB_STATE]
                    n_im = a_re * h_im + a_im * h_re + bu[q % 2, j, r, SLAB_STATE:2 * SLAB_STATE]
                    h_re, h_im = n_re, n_im
                    pair.append(jnp.concatenate([n_re, n_im], axis=1))
                hbuf[q % 2, j, t * nb:(t + 2) * nb, :] = jnp.concatenate(pair, axis=0).astype(BF16)
            st_re[:, lanes] = h_re
            st_im[:, lanes] = h_im

    def project_out(q):
        for j in range(SLABS):
            r = slice(q * sub_rows, (q + 1) * sub_rows)
            d_j = dsk[:, j * LANES:(j + 1) * LANES]
            ys_tm[j, r, :] = _dot(hbuf[q % 2, j], cmat[j]) + d_j * s_tm[j, r, :]

    n_sub = rows // sub_rows
    project_in(0)
    for q in range(n_sub):
        if q + 1 < n_sub:
            project_in(q + 1)
        recur(q)
        if q >= 1:
            project_out(q - 1)
    project_out(n_sub - 1)

    def phase3(blk, carry):
        r0 = pl.multiple_of(blk * ROW_BLOCK, ROW_BLOCK)
        ys = jnp.concatenate(
            [jnp.concatenate([ys_tm[j, pl.ds(blk * per_block + k, CHUNK, stride=nb), :]
                              for j in range(SLABS)], axis=1)
             for k in range(per_block)], axis=0)
        pb = _dot(ys.astype(BF16), w_b[...])
        yb = pb[:, :D_MODEL] * jax.nn.sigmoid(pb[:, D_MODEL:])
        hn = hn_s[pl.ds(r0, ROW_BLOCK), :]
        ga = jax.nn.sigmoid(_dot(hn, w_in[:, OFF_GA:OFF_GA + D_MODEL]))
        gb = jax.nn.sigmoid(_dot(hn, w_in[:, OFF_GB:OFF_GB + D_MODEL]))
        merged = (ga * ya_s[pl.ds(r0, ROW_BLOCK), :] + gb * yb).astype(BF16)
        y = _dot(merged, w_out[...])
        for k in range(per_block):
            b = blk * per_block + k
            x1_ref[b] = x_ref[b] + y[k * CHUNK:(k + 1) * CHUNK]
        return carry

    lax.fori_loop(0, rows // ROW_BLOCK, phase3, 0, unroll=2)

    @pl.when(i == pl.num_programs(0) - 1)
    def _():
        hre_o[...] = st_re[...]
        him_o[...] = st_im[...]


def _mixer_prompt(x, gmix, w_in, lng, lnb, wsp, bsp, are, aim, bmat, cmat, dsk, w_a, w_b, w_out):
    nb, seq, d = x.shape
    rows = nb * CHUNK
    consts = (gmix, w_in, lng, lnb, wsp, bsp, are, aim, bmat, cmat, dsk, w_a, w_b, w_out)
    x_spec = pl.BlockSpec((nb, CHUNK, d), lambda i: (0, i, 0))
    st_spec = pl.BlockSpec((nb, STATE_W), lambda i: (0, 0))
    return pl.pallas_call(
        _mixer_prompt_kernel,
        grid=(seq // CHUNK,),
        in_specs=[x_spec] + [_const_spec(c.shape) for c in consts],
        out_specs=(x_spec, st_spec, st_spec),
        out_shape=(jax.ShapeDtypeStruct(x.shape, F32),
                   jax.ShapeDtypeStruct((nb, STATE_W), F32),
                   jax.ShapeDtypeStruct((nb, STATE_W), F32)),
        scratch_shapes=[
            pltpu.VMEM((SLABS, rows, LANES), F32),
            pltpu.VMEM((SLABS, rows, LANES), F32),
            pltpu.VMEM((rows, d), F32),
            pltpu.VMEM((rows, d), BF16),
            pltpu.VMEM((2, SLABS, SCAN_STEPS * nb, 2 * SLAB_STATE), F32),
            pltpu.VMEM((2, SLABS, SCAN_STEPS * nb, 2 * SLAB_STATE), BF16),
            pltpu.VMEM((nb, STATE_W), F32),
            pltpu.VMEM((nb, STATE_W), F32),
        ],
        compiler_params=pltpu.CompilerParams(dimension_semantics=("arbitrary",),
                                             vmem_limit_bytes=VMEM_LIMIT),
        name="mixer_prompt",
    )(x, *consts)


def _mixer_sample_kernel(x_ref, h0re, h0im, gmix, w_in, lng, lnb, spw, bsp, are, aim, bmat, cmat,
                         dsk, w_a, w_b, w_out,
                         x1_ref, hre_o, him_o, v_o):
    x = x_ref[:, 0, :]
    hn = _rmsnorm(x, gmix[...]).astype(BF16)
    u = jax.nn.gelu(_dot(hn, w_in[:, OFF_U:OFF_U + D_MODEL]))
    v = _layernorm(jax.nn.gelu(_dot(hn, w_in[:, OFF_V:OFF_V + D_MODEL])), lng[...], lnb[...])
    v_o[:, 0, :] = v
    mixed = v * spw[...] + bsp[0:1, :]
    ya = _dot((u * mixed).astype(BF16), w_a[...])
    s = _dot(hn, w_in[:, OFF_S:OFF_S + SSM_WIDTH])
    ys = []
    for j in range(SLABS):
        lanes = slice(j * SLAB_STATE, (j + 1) * SLAB_STATE)
        s_j = s[:, j * LANES:(j + 1) * LANES]
        bu = _dot(s_j.astype(BF16), bmat[j])
        a_re, a_im = are[0:1, lanes], aim[0:1, lanes]
        p_re, p_im = h0re[:, lanes], h0im[:, lanes]
        n_re = a_re * p_re - a_im * p_im + bu[:, 0:SLAB_STATE]
        n_im = a_re * p_im + a_im * p_re + bu[:, SLAB_STATE:]
        hre_o[:, lanes] = n_re
        him_o[:, lanes] = n_im
        h = jnp.concatenate([n_re, n_im], axis=1).astype(BF16)
        ys.append(_dot(h, cmat[j]) + dsk[:, j * LANES:(j + 1) * LANES] * s_j)
    pb = _dot(jnp.concatenate(ys, axis=1).astype(BF16), w_b[...])
    yb = pb[:, :D_MODEL] * jax.nn.sigmoid(pb[:, D_MODEL:])
    ga = jax.nn.sigmoid(_dot(hn, w_in[:, OFF_GA:OFF_GA + D_MODEL]))
    gb = jax.nn.sigmoid(_dot(hn, w_in[:, OFF_GB:OFF_GB + D_MODEL]))
    merged = (ga * ya + gb * yb).astype(BF16)
    x1_ref[...] = x + _dot(merged, w_out[...])


def _mixer_sample(x, h0re, h0im, gmix, w_in, lng, lnb, spw, bsp, are, aim, bmat, cmat, dsk,
                  w_a, w_b, w_out):
    n, _, d = x.shape
    return pl.pallas_call(
        _mixer_sample_kernel,
        out_shape=(jax.ShapeDtypeStruct((n, d), F32),
                   jax.ShapeDtypeStruct((n, STATE_W), F32),
                   jax.ShapeDtypeStruct((n, STATE_W), F32),
                   jax.ShapeDtypeStruct((n, 1, d), F32)),
        compiler_params=pltpu.CompilerParams(vmem_limit_bytes=VMEM_LIMIT),
        name="mixer_sample",
    )(x, h0re, h0im, gmix, w_in, lng, lnb, spw, bsp, are, aim, bmat, cmat, dsk, w_a, w_b, w_out)


def _ffn_rows(x, gffn, w_gate, w_up, w_down, gfin):
    h = _rmsnorm(x, gffn[...]).astype(BF16)
    act = (jax.nn.silu(_dot(h, w_gate[...])) * _dot(h, w_up[...])).astype(BF16)
    x2 = x + _dot(act, w_down[...])
    return _rmsnorm(x2, gfin[...])


def _ffn_kernel(x_ref, xs_ref, gffn, wg_hbm, wu_hbm, wd_hbm, gfin, y_ref, ys_ref,
                w_gate, w_up, w_down, stage, sems):
    @pl.when(pl.program_id(0) == 0)
    def _():
        pieces = (_cast_pieces(wg_hbm, w_gate) + _cast_pieces(wu_hbm, w_up)
                  + _cast_pieces(wd_hbm, w_down))
        _load_cast(pieces, lambda slot, width: stage.at[slot, :, 0:width], sems)

    params = (gffn, w_gate, w_up, w_down, gfin)
    y_ref[...] = _ffn_rows(x_ref[...], *params)

    @pl.when(pl.program_id(0) == pl.num_programs(0) - 1)
    def _():
        ys_ref[:, 0, :] = _ffn_rows(xs_ref[...], *params)


def _ffn(x, xs, gffn, w_gate, w_up, w_down, gfin, block_rows):
    n, d = x.shape
    ns = xs.shape[0]
    row_spec = pl.BlockSpec((block_rows, d), lambda i: (i, 0))
    hbm_spec = pl.BlockSpec(memory_space=pl.ANY)
    return pl.pallas_call(
        _ffn_kernel,
        grid=(n // block_rows,),
        in_specs=[row_spec, _const_spec(xs.shape), _const_spec(gffn.shape), hbm_spec, hbm_spec,
                  hbm_spec, _const_spec(gfin.shape)],
        out_specs=(row_spec, pl.BlockSpec((ns, 1, d), lambda i: (0, 0, 0))),
        out_shape=(jax.ShapeDtypeStruct((n, d), F32), jax.ShapeDtypeStruct((ns, 1, d), F32)),
        scratch_shapes=[pltpu.VMEM(w_gate.shape, BF16), pltpu.VMEM(w_up.shape, BF16),
                        pltpu.VMEM(w_down.shape, BF16),
                        pltpu.VMEM((STAGE_SLOTS, STAGE_ROWS, STAGE_COLS), F32),
                        pltpu.SemaphoreType.DMA((STAGE_SLOTS,))],
        compiler_params=pltpu.CompilerParams(dimension_semantics=("arbitrary",),
                                             vmem_limit_bytes=VMEM_LIMIT),
        name="ffn",
    )(x, xs, gffn, w_gate, w_up, w_down, gfin)


def kernel(x_prompt, x_sample, state_ssm_re, state_ssm_im, norm_mix_g, w_in, ln_v_g, ln_v_b,
           w_spatial, b_spatial, ssm_lam_re, ssm_lam_im, ssm_log_dt, ssm_b_re, ssm_b_im,
           ssm_c_re, ssm_c_im, ssm_d, w_branch_a, w_branch_b, w_out, norm_ffn_g,
           w_gate_ffn, w_up_ffn, w_down_ffn, norm_final_g):
    depth = w_in.shape[0]
    assert depth == 1
    nb, seq, d = x_prompt.shape
    ns = x_sample.shape[0]
    row = lambda a: a.reshape(1, -1)

    xp = x_prompt
    xs = x_sample
    gfin = row(norm_final_g)
    p_re, p_im, s_re, s_im, s_v = [], [], [], [], []
    for l in range(depth):
        are, aim, bmat, cmat, wsp, bsp, spw = _prep(
            ssm_lam_re[l], ssm_lam_im[l], ssm_log_dt[l], ssm_b_re[l], ssm_b_im[l],
            ssm_c_re[l], ssm_c_im[l], w_spatial[l], b_spatial[l])
        dsk = row(ssm_d[l])
        shared = dict(gmix=row(norm_mix_g[l]), w_in=w_in[l].astype(BF16), lng=row(ln_v_g[l]),
                      lnb=row(ln_v_b[l]))
        tail = dict(are=are, aim=aim, bmat=bmat, cmat=cmat, dsk=dsk,
                    w_a=w_branch_a[l].astype(BF16), w_b=w_branch_b[l].astype(BF16),
                    w_out=w_out[l].astype(BF16))
        ffn_w = (row(norm_ffn_g[l]), w_gate_ffn[l], w_up_ffn[l], w_down_ffn[l])

        xp, hre, him = _mixer_prompt(xp, **shared, wsp=wsp, bsp=bsp, **tail)
        p_re.append(hre.reshape(nb, SSM_GROUPS, SSM_STATE))
        p_im.append(him.reshape(nb, SSM_GROUPS, SSM_STATE))

        xs, hre, him, v_rows = _mixer_sample(
            xs, state_ssm_re[l].reshape(ns, STATE_W), state_ssm_im[l].reshape(ns, STATE_W),
            **shared, spw=spw, bsp=bsp, **tail)
        s_re.append(hre.reshape(ns, SSM_GROUPS, SSM_STATE))
        s_im.append(him.reshape(ns, SSM_GROUPS, SSM_STATE))
        s_v.append(v_rows)

        xp, xs = _ffn(xp.reshape(nb * seq, d), xs, *ffn_w, gfin, 512)
        xp = xp.reshape(nb, seq, d)

    return (xp, xs, jnp.stack(p_re), jnp.stack(p_im), jnp.stack(s_re),
            jnp.stack(s_im), jnp.stack(s_v))
```

```python
import functools

import jax
import jax.numpy as jnp
from jax import lax
from jax.experimental import pallas as pl
from jax.experimental.pallas import tpu as pltpu

F32 = jnp.float32
BF16 = jnp.bfloat16

D_MODEL = 1024
CHUNK = 128
GMLP_GROUPS = 8
GROUP_CH = D_MODEL // GMLP_GROUPS
SSM_WIDTH = 512
SSM_GROUPS = 32
SSM_STATE = 64
SSM_GROUP_CH = 16
LANES = 128
SUBLANES = 8
SLABS = SSM_WIDTH // LANES
SLAB_GROUPS = LANES // SSM_GROUP_CH
SLAB_STATE = SLAB_GROUPS * SSM_STATE
STATE_W = SSM_GROUPS * SSM_STATE
IN_WIDTH = 2 * D_MODEL + SSM_WIDTH + 2 * D_MODEL
D_FF = 2816
EPS = 1e-6

OFF_U, OFF_V, OFF_S = 0, D_MODEL, 2 * D_MODEL
OFF_GA = 2 * D_MODEL + SSM_WIDTH
OFF_GB = OFF_GA + D_MODEL

ROW_BLOCK = 256
SCAN_STEPS = 16
FFN_BLOCK_ROWS = 1024
FFN_SUB_ROWS = 256
VMEM_LIMIT = 56 * 1024 * 1024


def _dot(a, b):
    return jnp.dot(a, b, preferred_element_type=F32)


def _rmsnorm(x, g):
    return x * lax.rsqrt(jnp.mean(x * x, axis=-1, keepdims=True) + EPS) * g


def _layernorm(x, g, b):
    mu = jnp.mean(x, axis=-1, keepdims=True)
    xc = x - mu
    var = jnp.mean(xc * xc, axis=-1, keepdims=True)
    return xc * lax.rsqrt(var + EPS) * g + b


def _const_spec(shape):
    zeros = (0,) * len(shape)
    return pl.BlockSpec(shape, lambda *_: zeros, pipeline_mode=pl.Buffered(1))


STAGE_SLOTS = 4
STAGE_ROWS, STAGE_COLS = 256, 1024


def _cast_pieces(src_hbm, dst_vmem):
    n_rows, n_cols = src_hbm.shape
    assert n_rows % STAGE_ROWS == 0 and n_cols % LANES == 0
    pieces = []
    for r in range(0, n_rows, STAGE_ROWS):
        for c in range(0, n_cols, STAGE_COLS):
            w = min(STAGE_COLS, n_cols - c)
            pieces.append((src_hbm.at[r:r + STAGE_ROWS, c:c + w],
                           dst_vmem.at[r:r + STAGE_ROWS, c:c + w], w))
    return pieces


def _load_cast(pieces, stage_view, sems):
    def copy(k):
        src, _, width = pieces[k]
        return pltpu.make_async_copy(src, stage_view(k % STAGE_SLOTS, width), sems.at[k % STAGE_SLOTS])

    for k in range(min(STAGE_SLOTS, len(pieces))):
        copy(k).start()
    for k, (_, dst, width) in enumerate(pieces):
        copy(k).wait()
        dst[...] = stage_view(k % STAGE_SLOTS, width)[...].astype(BF16)
        if k + STAGE_SLOTS < len(pieces):
            copy(k + STAGE_SLOTS).start()


def _dot_exact(a, b):
    return jnp.dot(a, b, preferred_element_type=F32, precision=lax.Precision.HIGHEST)


def _group_mask(shape, row_div, lane_div):
    rows = lax.broadcasted_iota(jnp.int32, shape, 0) // row_div
    lanes = lax.broadcasted_iota(jnp.int32, shape, 1) // lane_div
    return rows == lanes


def _prep_kernel(lre, lim, ldt, bre, bim, cre_in, cim_in, wsp_in, bsp_in,
                 are_o, aim_o, bmat_o, cmat_o, wsp_o, bsp_o, spw_o):
    def slab_rows(row_of_group):
        return jnp.concatenate(
            [jnp.concatenate([row_of_group(j * SLAB_GROUPS + k) for k in range(SLAB_GROUPS)], axis=1)
             for j in range(SLABS)], axis=0)

    lr = slab_rows(lambda g: lre[g:g + 1, :])
    li = slab_rows(lambda g: lim[g:g + 1, :])
    dt = jnp.exp(slab_rows(lambda g: jnp.broadcast_to(ldt[0:1, g:g + 1], (1, SSM_STATE))))
    mag = jnp.exp(lr * dt)
    a_re, a_im = mag * jnp.cos(li * dt), mag * jnp.sin(li * dt)
    n_re, n_im = a_re - 1.0, a_im
    den = lr * lr + li * li
    k_re = (n_re * lr + n_im * li) / den
    k_im = (n_im * lr - n_re * li) / den

    rep_h = (lax.broadcasted_iota(jnp.int32, (SSM_GROUP_CH, LANES), 1) % SSM_GROUP_CH
             == lax.broadcasted_iota(jnp.int32, (SSM_GROUP_CH, LANES), 0)).astype(F32)
    rep_p = (lax.broadcasted_iota(jnp.int32, (SSM_STATE, SLAB_STATE), 1) % SSM_STATE
             == lax.broadcasted_iota(jnp.int32, (SSM_STATE, SLAB_STATE), 0)).astype(F32)
    tb_re = _dot_exact(bre[...], rep_h)
    tb_im = _dot_exact(bim[...], rep_h)
    tc_re = _dot_exact(cre_in[...], rep_p)
    tc_im = _dot_exact(cim_in[...], rep_p)
    mask_b = _group_mask((SLAB_STATE, LANES), SSM_STATE, SSM_GROUP_CH)
    mask_c = _group_mask((LANES, SLAB_STATE), SSM_GROUP_CH, SSM_STATE)

    for j in range(SLABS):
        lanes = slice(j * SLAB_STATE, (j + 1) * SLAB_STATE)
        are_o[:, lanes] = jnp.broadcast_to(a_re[j:j + 1, :], (SUBLANES, SLAB_STATE))
        aim_o[:, lanes] = jnp.broadcast_to(a_im[j:j + 1, :], (SUBLANES, SLAB_STATE))
        rows = slice(j * SLAB_STATE, (j + 1) * SLAB_STATE)
        m_re = jnp.where(mask_b, tb_re[rows], 0.0).T
        m_im = jnp.where(mask_b, tb_im[rows], 0.0).T
        c_re, c_im = k_re[j:j + 1, :], k_im[j:j + 1, :]
        bmat_o[j] = jnp.concatenate([c_re * m_re - c_im * m_im, c_re * m_im + c_im * m_re],
                                    axis=1).astype(BF16)
        rows = slice(j * LANES, (j + 1) * LANES)
        read = jnp.concatenate([jnp.where(mask_c, tc_re[rows], 0.0),
                                jnp.where(mask_c, -tc_im[rows], 0.0)], axis=1)
        cmat_o[j] = read.T.astype(BF16)

    tril = (lax.broadcasted_iota(jnp.int32, (CHUNK, CHUNK), 0)
            >= lax.broadcasted_iota(jnp.int32, (CHUNK, CHUNK), 1))
    spw = []
    for g in range(GMLP_GROUPS):
        w = wsp_in[g]
        wsp_o[g] = jnp.where(tril, w, 0.0).astype(BF16)
        spw.append(jnp.broadcast_to(w[0:1, 0:1], (1, GROUP_CH)))
    spw_o[...] = jnp.concatenate(spw, axis=1)
    sel = _group_mask((D_MODEL, GMLP_GROUPS), GROUP_CH, 1).astype(F32)
    bsp_o[...] = _dot_exact(sel, bsp_in[...]).T


def _prep(lam_re, lam_im, log_dt, b_re, b_im, c_re, c_im, w_spatial, b_spatial):
    g, p, h = b_re.shape
    return pl.pallas_call(
        _prep_kernel,
        out_shape=(jax.ShapeDtypeStruct((SUBLANES, STATE_W), F32),
                   jax.ShapeDtypeStruct((SUBLANES, STATE_W), F32),
                   jax.ShapeDtypeStruct((SLABS, LANES, 2 * SLAB_STATE), BF16),
                   jax.ShapeDtypeStruct((SLABS, 2 * SLAB_STATE, LANES), BF16),
                   jax.ShapeDtypeStruct(w_spatial.shape, BF16),
                   jax.ShapeDtypeStruct((CHUNK, D_MODEL), F32),
                   jax.ShapeDtypeStruct((1, D_MODEL), F32)),
        name="prep",
    )(lam_re, lam_im, log_dt.reshape(1, g), b_re.reshape(g * p, h), b_im.reshape(g * p, h),
      c_re.reshape(g * h, p), c_im.reshape(g * h, p), w_spatial, b_spatial)


def _mixer_prompt_kernel(x_ref, gmix, w_in, lng, lnb, wsp, bsp, are, aim, bmat, cmat, dsk,
                         w_a, w_b, w_out,
                         x1_ref, hre_o, him_o,
                         s_tm, ys_tm, ya_s, hn_s, bu, hbuf, st_re, st_im):
    i = pl.program_id(0)
    nb = x_ref.shape[0]
    rows = nb * CHUNK
    per_block = ROW_BLOCK // CHUNK

    @pl.when(i == 0)
    def _():
        st_re[...] = jnp.zeros_like(st_re)
        st_im[...] = jnp.zeros_like(st_im)

    def project(blk):
        r = slice(blk * ROW_BLOCK, (blk + 1) * ROW_BLOCK)
        x = jnp.concatenate([x_ref[blk * per_block + k] for k in range(per_block)], axis=0)
        hn = _rmsnorm(x, gmix[...]).astype(BF16)
        hn_s[r, :] = hn
        zu = _dot(hn, w_in[:, OFF_U:OFF_U + D_MODEL])
        zv = _dot(hn, w_in[:, OFF_V:OFF_V + D_MODEL])
        s = _dot(hn, w_in[:, OFF_S:OFF_S + SSM_WIDTH])
        for k in range(per_block):
            b = blk * per_block + k
            for j in range(SLABS):
                s_tm[j, pl.ds(b, CHUNK, stride=nb), :] = s[k * CHUNK:(k + 1) * CHUNK,
                                                          j * LANES:(j + 1) * LANES]
        return zu, zv

    def gate(blk, zu, zv):
        u = jax.nn.gelu(zu)
        vb = _layernorm(jax.nn.gelu(zv), lng[...], lnb[...]).astype(BF16)
        mixed = []
        for k in range(per_block):
            cols = [_dot(wsp[g], vb[k * CHUNK:(k + 1) * CHUNK, g * GROUP_CH:(g + 1) * GROUP_CH])
                    for g in range(GMLP_GROUPS)]
            mixed.append(jnp.concatenate(cols, axis=1) + bsp[...])
        gated = (u * jnp.concatenate(mixed, axis=0)).astype(BF16)
        ya_s[blk * ROW_BLOCK:(blk + 1) * ROW_BLOCK, :] = _dot(gated, w_a[...])

    n_blk = rows // ROW_BLOCK
    z = project(0)
    for blk in range(n_blk):
        z_next = project(blk + 1) if blk + 1 < n_blk else None
        gate(blk, *z)
        z = z_next

    sub_rows = SCAN_STEPS * nb

    def project_in(q):
        for j in range(SLABS):
            s_blk = s_tm[j, q * sub_rows:(q + 1) * sub_rows, :]
            bu[q % 2, j] = _dot(s_blk.astype(BF16), bmat[j])

    def recur(q):
        for j in range(SLABS):
            lanes = slice(j * SLAB_STATE, (j + 1) * SLAB_STATE)
            a_re, a_im = are[:, lanes], aim[:, lanes]
            h_re, h_im = st_re[:, lanes], st_im[:, lanes]
            for t in range(0, SCAN_STEPS, 2):
                pair = []
                for r in (slice(t * nb, (t + 1) * nb), slice((t + 1) * nb, (t + 2) * nb)):
                    n_re = a_re * h_re - a_im * h_im + bu[q % 2, j, r, 0:SLAB_STATE]
                    n_im = a_re * h_im + a_im * h_re + bu[q % 2, j, r, SLAB_STATE:2 * SLAB_STATE]
                    h_re, h_im = n_re, n_im
                    pair.append(jnp.concatenate([n_re, n_im], axis=1))
                hbuf[q % 2, j, t * nb:(t + 2) * nb, :] = jnp.concatenate(pair, axis=0).astype(BF16)
            st_re[:, lanes] = h_re
            st_im[:, lanes] = h_im

    def project_out(q):
        for j in range(SLABS):
            r = slice(q * sub_rows, (q + 1) * sub_rows)
            d_j = dsk[:, j * LANES:(j + 1) * LANES]
            ys_tm[j, r, :] = _dot(hbuf[q % 2, j], cmat[j]) + d_j * s_tm[j, r, :]

    n_sub = rows // sub_rows
    project_in(0)
    for q in range(n_sub):
        if q + 1 < n_sub:
            project_in(q + 1)
        recur(q)
        if q >= 1:
            project_out(q - 1)
    project_out(n_sub - 1)

    def phase3(blk, carry):
        r0 = pl.multiple_of(blk * ROW_BLOCK, ROW_BLOCK)
        ys = jnp.concatenate(
            [jnp.concatenate([ys_tm[j, pl.ds(blk * per_block + k, CHUNK, stride=nb), :]
                              for j in range(SLABS)], axis=1)
             for k in range(per_block)], axis=0)
        pb = _dot(ys.astype(BF16), w_b[...])
        yb = pb[:, :D_MODEL] * jax.nn.sigmoid(pb[:, D_MODEL:])
        hn = hn_s[pl.ds(r0, ROW_BLOCK), :]
        ga = jax.nn.sigmoid(_dot(hn, w_in[:, OFF_GA:OFF_GA + D_MODEL]))
        gb = jax.nn.sigmoid(_dot(hn, w_in[:, OFF_GB:OFF_GB + D_MODEL]))
        merged = (ga * ya_s[pl.ds(r0, ROW_BLOCK), :] + gb * yb).astype(BF16)
        y = _dot(merged, w_out[...])
        for k in range(per_block):
            b = blk * per_block + k
            x1_ref[b] = x_ref[b] + y[k * CHUNK:(k + 1) * CHUNK]
        return carry

    lax.fori_loop(0, rows // ROW_BLOCK, phase3, 0, unroll=2)

    @pl.when(i == pl.num_programs(0) - 1)
    def _():
        hre_o[...] = st_re[...]
        him_o[...] = st_im[...]


def _mixer_prompt(x, gmix, w_in, lng, lnb, wsp, bsp, are, aim, bmat, cmat, dsk, w_a, w_b, w_out):
    nb, seq, d = x.shape
    rows = nb * CHUNK
    consts = (gmix, w_in, lng, lnb, wsp, bsp, are, aim, bmat, cmat, dsk, w_a, w_b, w_out)
    x_spec = pl.BlockSpec((nb, CHUNK, d), lambda i: (0, i, 0))
    st_spec = pl.BlockSpec((nb, STATE_W), lambda i: (0, 0))
    return pl.pallas_call(
        _mixer_prompt_kernel,
        grid=(seq // CHUNK,),
        in_specs=[x_spec] + [_const_spec(c.shape) for c in consts],
        out_specs=(x_spec, st_spec, st_spec),
        out_shape=(jax.ShapeDtypeStruct(x.shape, F32),
                   jax.ShapeDtypeStruct((nb, STATE_W), F32),
                   jax.ShapeDtypeStruct((nb, STATE_W), F32)),
        scratch_shapes=[
            pltpu.VMEM((SLABS, rows, LANES), F32),
            pltpu.VMEM((SLABS, rows, LANES), F32),
            pltpu.VMEM((rows, d), F32),
            pltpu.VMEM((rows, d), BF16),
            pltpu.VMEM((2, SLABS, SCAN_STEPS * nb, 2 * SLAB_STATE), F32),
            pltpu.VMEM((2, SLABS, SCAN_STEPS * nb, 2 * SLAB_STATE), BF16),
            pltpu.VMEM((nb, STATE_W), F32),
            pltpu.VMEM((nb, STATE_W), F32),
        ],
        compiler_params=pltpu.CompilerParams(dimension_semantics=("arbitrary",),
                                             vmem_limit_bytes=VMEM_LIMIT),
        name="mixer_prompt",
    )(x, *consts)


def _mixer_sample_kernel(x_ref, h0re, h0im, gmix, w_in, lng, lnb, spw, bsp, are, aim, bmat, cmat,
                         dsk, w_a, w_b, w_out,
                         x1_ref, hre_o, him_o, v_o):
    x = x_ref[:, 0, :]
    hn = _rmsnorm(x, gmix[...]).astype(BF16)
    u = jax.nn.gelu(_dot(hn, w_in[:, OFF_U:OFF_U + D_MODEL]))
    v = _layernorm(jax.nn.gelu(_dot(hn, w_in[:, OFF_V:OFF_V + D_MODEL])), lng[...], lnb[...])
    v_o[:, 0, :] = v
    mixed = v * spw[...] + bsp[0:1, :]
    ya = _dot((u * mixed).astype(BF16), w_a[...])
    s = _dot(hn, w_in[:, OFF_S:OFF_S + SSM_WIDTH])
    ys = []
    for j in range(SLABS):
        lanes = slice(j * SLAB_STATE, (j + 1) * SLAB_STATE)
        s_j = s[:, j * LANES:(j + 1) * LANES]
        bu = _dot(s_j.astype(BF16), bmat[j])
        a_re, a_im = are[0:1, lanes], aim[0:1, lanes]
        p_re, p_im = h0re[:, lanes], h0im[:, lanes]
        n_re = a_re * p_re - a_im * p_im + bu[:, 0:SLAB_STATE]
        n_im = a_re * p_im + a_im * p_re + bu[:, SLAB_STATE:]
        hre_o[:, lanes] = n_re
        him_o[:, lanes] = n_im
        h = jnp.concatenate([n_re, n_im], axis=1).astype(BF16)
        ys.append(_dot(h, cmat[j]) + dsk[:, j * LANES:(j + 1) * LANES] * s_j)
    pb = _dot(jnp.concatenate(ys, axis=1).astype(BF16), w_b[...])
    yb = pb[:, :D_MODEL] * jax.nn.sigmoid(pb[:, D_MODEL:])
    ga = jax.nn.sigmoid(_dot(hn, w_in[:, OFF_GA:OFF_GA + D_MODEL]))
    gb = jax.nn.sigmoid(_dot(hn, w_in[:, OFF_GB:OFF_GB + D_MODEL]))
    merged = (ga * ya + gb * yb).astype(BF16)
    x1_ref[...] = x + _dot(merged, w_out[...])


def _mixer_sample(x, h0re, h0im, gmix, w_in, lng, lnb, spw, bsp, are, aim, bmat, cmat, dsk,
                  w_a, w_b, w_out):
    n, _, d = x.shape
    return pl.pallas_call(
        _mixer_sample_kernel,
        out_shape=(jax.ShapeDtypeStruct((n, d), F32),
                   jax.ShapeDtypeStruct((n, STATE_W), F32),
                   jax.ShapeDtypeStruct((n, STATE_W), F32),
                   jax.ShapeDtypeStruct((n, 1, d), F32)),
        compiler_params=pltpu.CompilerParams(vmem_limit_bytes=VMEM_LIMIT),
        name="mixer_sample",
    )(x, h0re, h0im, gmix, w_in, lng, lnb, spw, bsp, are, aim, bmat, cmat, dsk, w_a, w_b, w_out)


def _ffn_rows(x, gffn, w_gate, w_up, w_down, gfin):
    h = _rmsnorm(x, gffn[...]).astype(BF16)
    act = (jax.nn.silu(_dot(h, w_gate[...])) * _dot(h, w_up[...])).astype(BF16)
    x2 = x + _dot(act, w_down[...])
    return _rmsnorm(x2, gfin[...])


def _ffn_kernel(x_ref, xs_ref, gffn, wg_hbm, wu_hbm, wd_hbm, gfin, y_ref, ys_ref,
                w_gate, w_up, w_down, stage, sems):
    @pl.when(pl.program_id(0) == 0)
    def _():
        pieces = (_cast_pieces(wg_hbm, w_gate) + _cast_pieces(wu_hbm, w_up)
                  + _cast_pieces(wd_hbm, w_down))
        _load_cast(pieces, lambda slot, width: stage.at[slot, :, 0:width], sems)

    params = (gffn, w_gate, w_up, w_down, gfin)
    for r0 in range(0, x_ref.shape[0], FFN_SUB_ROWS):
        r = slice(r0, r0 + FFN_SUB_ROWS)
        y_ref[r, :] = _ffn_rows(x_ref[r, :], *params)

    @pl.when(pl.program_id(0) == pl.num_programs(0) - 1)
    def _():
        ys_ref[:, 0, :] = _ffn_rows(xs_ref[...], *params)


def _ffn(x, xs, gffn, w_gate, w_up, w_down, gfin, block_rows):
    n, d = x.shape
    ns = xs.shape[0]
    row_spec = pl.BlockSpec((block_rows, d), lambda i: (i, 0))
    hbm_spec = pl.BlockSpec(memory_space=pl.ANY)
    return pl.pallas_call(
        _ffn_kernel,
        grid=(n // block_rows,),
        in_specs=[row_spec, _const_spec(xs.shape), _const_spec(gffn.shape), hbm_spec, hbm_spec,
                  hbm_spec, _const_spec(gfin.shape)],
        out_specs=(row_spec, pl.BlockSpec((ns, 1, d), lambda i: (0, 0, 0))),
        out_shape=(jax.ShapeDtypeStruct((n, d), F32), jax.ShapeDtypeStruct((ns, 1, d), F32)),
        scratch_shapes=[pltpu.VMEM(w_gate.shape, BF16), pltpu.VMEM(w_up.shape, BF16),
                        pltpu.VMEM(w_down.shape, BF16),
                        pltpu.VMEM((STAGE_SLOTS, STAGE_ROWS, STAGE_COLS), F32),
                        pltpu.SemaphoreType.DMA((STAGE_SLOTS,))],
        compiler_params=pltpu.CompilerParams(dimension_semantics=("arbitrary",),
                                             vmem_limit_bytes=VMEM_LIMIT),
        name="ffn",
    )(x, xs, gffn, w_gate, w_up, w_down, gfin)


def kernel(x_prompt, x_sample, state_ssm_re, state_ssm_im, norm_mix_g, w_in, ln_v_g, ln_v_b,
           w_spatial, b_spatial, ssm_lam_re, ssm_lam_im, ssm_log_dt, ssm_b_re, ssm_b_im,
           ssm_c_re, ssm_c_im, ssm_d, w_branch_a, w_branch_b, w_out, norm_ffn_g,
           w_gate_ffn, w_up_ffn, w_down_ffn, norm_final_g):
    depth = w_in.shape[0]
    assert depth == 1
    nb, seq, d = x_prompt.shape
    ns = x_sample.shape[0]
    row = lambda a: a.reshape(1, -1)

    xp = x_prompt
    xs = x_sample
    gfin = row(norm_final_g)
    p_re, p_im, s_re, s_im, s_v = [], [], [], [], []
    for l in range(depth):
        are, aim, bmat, cmat, wsp, bsp, spw = _prep(
            ssm_lam_re[l], ssm_lam_im[l], ssm_log_dt[l], ssm_b_re[l], ssm_b_im[l],
            ssm_c_re[l], ssm_c_im[l], w_spatial[l], b_spatial[l])
        dsk = row(ssm_d[l])
        shared = dict(gmix=row(norm_mix_g[l]), w_in=w_in[l].astype(BF16), lng=row(ln_v_g[l]),
                      lnb=row(ln_v_b[l]))
        tail = dict(are=are, aim=aim, bmat=bmat, cmat=cmat, dsk=dsk,
                    w_a=w_branch_a[l].astype(BF16), w_b=w_branch_b[l].astype(BF16),
                    w_out=w_out[l].astype(BF16))
        ffn_w = (row(norm_ffn_g[l]), w_gate_ffn[l], w_up_ffn[l], w_down_ffn[l])

        xp, hre, him = _mixer_prompt(xp, **shared, wsp=wsp, bsp=bsp, **tail)
        p_re.append(hre.reshape(nb, SSM_GROUPS, SSM_STATE))
        p_im.append(him.reshape(nb, SSM_GROUPS, SSM_STATE))

        xs, hre, him, v_rows = _mixer_sample(
            xs, state_ssm_re[l].reshape(ns, STATE_W), state_ssm_im[l].reshape(ns, STATE_W),
            **shared, spw=spw, bsp=bsp, **tail)
        s_re.append(hre.reshape(ns, SSM_GROUPS, SSM_STATE))
        s_im.append(him.reshape(ns, SSM_GROUPS, SSM_STATE))
        s_v.append(v_rows)

        xp, xs = _ffn(xp.reshape(nb * seq, d), xs, *ffn_w, gfin, FFN_BLOCK_ROWS)
        xp = xp.reshape(nb, seq, d)

    return (xp, xs, jnp.stack(p_re), jnp.stack(p_im), jnp.stack(s_re),
            jnp.stack(s_im), jnp.stack(s_v))
```

```python
import functools

import jax
import jax.numpy as jnp
from jax import lax
from jax.experimental import pallas as pl
from jax.experimental.pallas import tpu as pltpu

F32 = jnp.float32
BF16 = jnp.bfloat16

D_MODEL = 1024
CHUNK = 128
GMLP_GROUPS = 8
GROUP_CH = D_MODEL // GMLP_GROUPS
SSM_WIDTH = 512
SSM_GROUPS = 32
SSM_STATE = 64
SSM_GROUP_CH = 16
LANES = 128
SUBLANES = 8
SLABS = SSM_WIDTH // LANES
SLAB_GROUPS = LANES // SSM_GROUP_CH
SLAB_STATE = SLAB_GROUPS * SSM_STATE
STATE_W = SSM_GROUPS * SSM_STATE
IN_WIDTH = 2 * D_MODEL + SSM_WIDTH + 2 * D_MODEL
D_FF = 2816
EPS = 1e-6

OFF_U, OFF_V, OFF_S = 0, D_MODEL, 2 * D_MODEL
OFF_GA = 2 * D_MODEL + SSM_WIDTH
OFF_GB = OFF_GA + D_MODEL

ROW_BLOCK = 256
SCAN_STEPS = 16
DEC_ROWS = 128
DEC_IO_COPIES = 6
FFN_BLOCK_ROWS = 1024
FFN_SUB_ROWS = 256
VMEM_LIMIT = 56 * 1024 * 1024


def _dot(a, b):
    return jnp.dot(a, b, preferred_element_type=F32)


def _rmsnorm(x, g):
    return x * lax.rsqrt(jnp.mean(x * x, axis=-1, keepdims=True) + EPS) * g


def _layernorm(x, g, b):
    mu = jnp.mean(x, axis=-1, keepdims=True)
    xc = x - mu
    var = jnp.mean(xc * xc, axis=-1, keepdims=True)
    return xc * lax.rsqrt(var + EPS) * g + b


def _const_spec(shape):
    zeros = (0,) * len(shape)
    return pl.BlockSpec(shape, lambda *_: zeros, pipeline_mode=pl.Buffered(1))


STAGE_SLOTS = 4
STAGE_ROWS, STAGE_COLS = 256, 1024


def _cast_pieces(src_hbm, dst_vmem):
    n_rows, n_cols = src_hbm.shape
    assert n_rows % STAGE_ROWS == 0 and n_cols % LANES == 0
    pieces = []
    for r in range(0, n_rows, STAGE_ROWS):
        for c in range(0, n_cols, STAGE_COLS):
            w = min(STAGE_COLS, n_cols - c)
            pieces.append((src_hbm.at[r:r + STAGE_ROWS, c:c + w],
                           dst_vmem.at[r:r + STAGE_ROWS, c:c + w], w))
    return pieces


def _load_cast(pieces, stage_view, sems):
    def copy(k):
        src, _, width = pieces[k]
        return pltpu.make_async_copy(src, stage_view(k % STAGE_SLOTS, width), sems.at[k % STAGE_SLOTS])

    for k in range(min(STAGE_SLOTS, len(pieces))):
        copy(k).start()
    for k, (_, dst, width) in enumerate(pieces):
        copy(k).wait()
        dst[...] = stage_view(k % STAGE_SLOTS, width)[...].astype(BF16)
        if k + STAGE_SLOTS < len(pieces):
            copy(k + STAGE_SLOTS).start()


def _dot_exact(a, b):
    return jnp.dot(a, b, preferred_element_type=F32, precision=lax.Precision.HIGHEST)


def _group_mask(shape, row_div, lane_div):
    rows = lax.broadcasted_iota(jnp.int32, shape, 0) // row_div
    lanes = lax.broadcasted_iota(jnp.int32, shape, 1) // lane_div
    return rows == lanes


def _prep_kernel(lre, lim, ldt, bre, bim, cre_in, cim_in, wsp_in, bsp_in,
                 are_o, aim_o, bmat_o, cmat_o, wsp_o, bsp_o, spw_o):
    def slab_rows(row_of_group):
        return jnp.concatenate(
            [jnp.concatenate([row_of_group(j * SLAB_GROUPS + k) for k in range(SLAB_GROUPS)], axis=1)
             for j in range(SLABS)], axis=0)

    lr = slab_rows(lambda g: lre[g:g + 1, :])
    li = slab_rows(lambda g: lim[g:g + 1, :])
    dt = jnp.exp(slab_rows(lambda g: jnp.broadcast_to(ldt[0:1, g:g + 1], (1, SSM_STATE))))
    mag = jnp.exp(lr * dt)
    a_re, a_im = mag * jnp.cos(li * dt), mag * jnp.sin(li * dt)
    n_re, n_im = a_re - 1.0, a_im
    den = lr * lr + li * li
    k_re = (n_re * lr + n_im * li) / den
    k_im = (n_im * lr - n_re * li) / den

    rep_h = (lax.broadcasted_iota(jnp.int32, (SSM_GROUP_CH, LANES), 1) % SSM_GROUP_CH
             == lax.broadcasted_iota(jnp.int32, (SSM_GROUP_CH, LANES), 0)).astype(F32)
    rep_p = (lax.broadcasted_iota(jnp.int32, (SSM_STATE, SLAB_STATE), 1) % SSM_STATE
             == lax.broadcasted_iota(jnp.int32, (SSM_STATE, SLAB_STATE), 0)).astype(F32)
    tb_re = _dot_exact(bre[...], rep_h)
    tb_im = _dot_exact(bim[...], rep_h)
    tc_re = _dot_exact(cre_in[...], rep_p)
    tc_im = _dot_exact(cim_in[...], rep_p)
    mask_b = _group_mask((SLAB_STATE, LANES), SSM_STATE, SSM_GROUP_CH)
    mask_c = _group_mask((LANES, SLAB_STATE), SSM_GROUP_CH, SSM_STATE)

    for j in range(SLABS):
        lanes = slice(j * SLAB_STATE, (j + 1) * SLAB_STATE)
        are_o[:, lanes] = jnp.broadcast_to(a_re[j:j + 1, :], (SUBLANES, SLAB_STATE))
        aim_o[:, lanes] = jnp.broadcast_to(a_im[j:j + 1, :], (SUBLANES, SLAB_STATE))
        rows = slice(j * SLAB_STATE, (j + 1) * SLAB_STATE)
        m_re = jnp.where(mask_b, tb_re[rows], 0.0).T
        m_im = jnp.where(mask_b, tb_im[rows], 0.0).T
        c_re, c_im = k_re[j:j + 1, :], k_im[j:j + 1, :]
        bmat_o[j] = jnp.concatenate([c_re * m_re - c_im * m_im, c_re * m_im + c_im * m_re],
                                    axis=1).astype(BF16)
        rows = slice(j * LANES, (j + 1) * LANES)
        read = jnp.concatenate([jnp.where(mask_c, tc_re[rows], 0.0),
                                jnp.where(mask_c, -tc_im[rows], 0.0)], axis=1)
        cmat_o[j] = read.T.astype(BF16)

    tril = (lax.broadcasted_iota(jnp.int32, (CHUNK, CHUNK), 0)
            >= lax.broadcasted_iota(jnp.int32, (CHUNK, CHUNK), 1))
    spw = []
    for g in range(GMLP_GROUPS):
        w = wsp_in[g]
        wsp_o[g] = jnp.where(tril, w, 0.0).astype(BF16)
        spw.append(jnp.broadcast_to(w[0:1, 0:1], (1, GROUP_CH)))
    spw_o[...] = jnp.concatenate(spw, axis=1)
    sel = _group_mask((D_MODEL, GMLP_GROUPS), GROUP_CH, 1).astype(F32)
    bsp_o[...] = _dot_exact(sel, bsp_in[...]).T


def _prep(lam_re, lam_im, log_dt, b_re, b_im, c_re, c_im, w_spatial, b_spatial):
    g, p, h = b_re.shape
    return pl.pallas_call(
        _prep_kernel,
        out_shape=(jax.ShapeDtypeStruct((SUBLANES, STATE_W), F32),
                   jax.ShapeDtypeStruct((SUBLANES, STATE_W), F32),
                   jax.ShapeDtypeStruct((SLABS, LANES, 2 * SLAB_STATE), BF16),
                   jax.ShapeDtypeStruct((SLABS, 2 * SLAB_STATE, LANES), BF16),
                   jax.ShapeDtypeStruct(w_spatial.shape, BF16),
                   jax.ShapeDtypeStruct((CHUNK, D_MODEL), F32),
                   jax.ShapeDtypeStruct((1, D_MODEL), F32)),
        name="prep",
    )(lam_re, lam_im, log_dt.reshape(1, g), b_re.reshape(g * p, h), b_im.reshape(g * p, h),
      c_re.reshape(g * h, p), c_im.reshape(g * h, p), w_spatial, b_spatial)


def _decode_rows(io, gmix, w_in, lng, lnb, spw, bsp, are, aim, bmat, cmat, dsk, w_a, w_b, w_out):
    n = DEC_ROWS
    x = io[0:n, :]
    hn = _rmsnorm(x, gmix[...]).astype(BF16)
    u = jax.nn.gelu(_dot(hn, w_in[:, OFF_U:OFF_U + D_MODEL]))
    v = _layernorm(jax.nn.gelu(_dot(hn, w_in[:, OFF_V:OFF_V + D_MODEL])), lng[...], lnb[...])
    io[5 * n:6 * n, :] = v
    mixed = v * spw[...] + bsp[0:1, :]
    ya = _dot((u * mixed).astype(BF16), w_a[...])
    s = _dot(hn, w_in[:, OFF_S:OFF_S + SSM_WIDTH])
    ys = []
    for j in range(SLABS):
        lanes = slice(j * SLAB_STATE, (j + 1) * SLAB_STATE)
        half, off = divmod(j * SLAB_STATE, D_MODEL)
        re_rows = slice((1 + half) * n, (2 + half) * n)
        im_rows = slice((3 + half) * n, (4 + half) * n)
        cols = slice(off, off + SLAB_STATE)
        s_j = s[:, j * LANES:(j + 1) * LANES]
        bu = _dot(s_j.astype(BF16), bmat[j])
        a_re, a_im = are[0:1, lanes], aim[0:1, lanes]
        p_re, p_im = io[re_rows, cols], io[im_rows, cols]
        n_re = a_re * p_re - a_im * p_im + bu[:, 0:SLAB_STATE]
        n_im = a_re * p_im + a_im * p_re + bu[:, SLAB_STATE:]
        io[re_rows, cols] = n_re
        io[im_rows, cols] = n_im
        h = jnp.concatenate([n_re, n_im], axis=1).astype(BF16)
        ys.append(_dot(h, cmat[j]) + dsk[:, j * LANES:(j + 1) * LANES] * s_j)
    pb = _dot(jnp.concatenate(ys, axis=1).astype(BF16), w_b[...])
    yb = pb[:, :D_MODEL] * jax.nn.sigmoid(pb[:, D_MODEL:])
    ga = jax.nn.sigmoid(_dot(hn, w_in[:, OFF_GA:OFF_GA + D_MODEL]))
    gb = jax.nn.sigmoid(_dot(hn, w_in[:, OFF_GB:OFF_GB + D_MODEL]))
    merged = (ga * ya + gb * yb).astype(BF16)
    io[6 * n:7 * n, :] = x + _dot(merged, w_out[...])


def _copy_all(pairs, sems):
    copies = [pltpu.make_async_copy(src, dst, sems.at[k]) for k, (src, dst) in enumerate(pairs)]
    for c in copies:
        c.start()
    for c in copies:
        c.wait()


def _mixer_kernel(x_ref, xs_hbm, h0re_hbm, h0im_hbm, gmix, w_in_hbm, lng, lnb, wsp, bsp, spw,
                  are, aim, bmat, cmat, dsk, w_a_hbm, w_b_hbm, w_out_hbm,
                  x1_ref, hre_o, him_o, xs1_hbm, hsre_hbm, hsim_hbm, vs_hbm,
                  s_tm, ys_tm, ya_s, hn_s, bu, hbuf, st_re, st_im,
                  w_in, w_a, w_b, w_out, stage_sems, io_sems):
    i = pl.program_id(0)
    nb = x_ref.shape[0]
    rows = nb * CHUNK
    per_block = ROW_BLOCK // CHUNK

    @pl.when(i == 0)
    def _():
        st_re[...] = jnp.zeros_like(st_re)
        st_im[...] = jnp.zeros_like(st_im)
        pieces = (_cast_pieces(w_in_hbm, w_in) + _cast_pieces(w_a_hbm, w_a)
                  + _cast_pieces(w_b_hbm, w_b) + _cast_pieces(w_out_hbm, w_out))
        _load_cast(pieces,
                   lambda slot, width: ya_s.at[slot * STAGE_ROWS:(slot + 1) * STAGE_ROWS, 0:width],
                   stage_sems)
        n, half = DEC_ROWS, STATE_W // 2
        blk = lambda k: ya_s.at[k * n:(k + 1) * n, :]
        _copy_all([(xs_hbm.at[:, 0, :], blk(0)),
                   (h0re_hbm.at[:, 0:half], blk(1)), (h0re_hbm.at[:, half:STATE_W], blk(2)),
                   (h0im_hbm.at[:, 0:half], blk(3)), (h0im_hbm.at[:, half:STATE_W], blk(4))],
                  io_sems)
        _decode_rows(ya_s, gmix, w_in, lng, lnb, spw, bsp, are, aim, bmat, cmat, dsk,
                     w_a, w_b, w_out)
        _copy_all([(blk(1), hsre_hbm.at[:, 0:half]), (blk(2), hsre_hbm.at[:, half:STATE_W]),
                   (blk(3), hsim_hbm.at[:, 0:half]), (blk(4), hsim_hbm.at[:, half:STATE_W]),
                   (blk(5), vs_hbm.at[:, 0, :]), (blk(6), xs1_hbm)],
                  io_sems)

    def project(blk):
        r = slice(blk * ROW_BLOCK, (blk + 1) * ROW_BLOCK)
        x = jnp.concatenate([x_ref[blk * per_block + k] for k in range(per_block)], axis=0)
        hn = _rmsnorm(x, gmix[...]).astype(BF16)
        hn_s[r, :] = hn
        zu = _dot(hn, w_in[:, OFF_U:OFF_U + D_MODEL])
        zv = _dot(hn, w_in[:, OFF_V:OFF_V + D_MODEL])
        s = _dot(hn, w_in[:, OFF_S:OFF_S + SSM_WIDTH])
        for k in range(per_block):
            b = blk * per_block + k
            for j in range(SLABS):
                s_tm[j, pl.ds(b, CHUNK, stride=nb), :] = s[k * CHUNK:(k + 1) * CHUNK,
                                                          j * LANES:(j + 1) * LANES]
        return zu, zv

    def gate(blk, zu, zv):
        u = jax.nn.gelu(zu)
        vb = _layernorm(jax.nn.gelu(zv), lng[...], lnb[...]).astype(BF16)
        mixed = []
        for k in range(per_block):
            cols = [_dot(wsp[g], vb[k * CHUNK:(k + 1) * CHUNK, g * GROUP_CH:(g + 1) * GROUP_CH])
                    for g in range(GMLP_GROUPS)]
            mixed.append(jnp.concatenate(cols, axis=1) + bsp[...])
        gated = (u * jnp.concatenate(mixed, axis=0)).astype(BF16)
        ya_s[blk * ROW_BLOCK:(blk + 1) * ROW_BLOCK, :] = _dot(gated, w_a[...])

    n_blk = rows // ROW_BLOCK
    z = project(0)
    for blk in range(n_blk):
        z_next = project(blk + 1) if blk + 1 < n_blk else None
        gate(blk, *z)
        z = z_next

    sub_rows = SCAN_STEPS * nb

    def project_in(q):
        for j in range(SLABS):
            s_blk = s_tm[j, q * sub_rows:(q + 1) * sub_rows, :]
            bu[q % 2, j] = _dot(s_blk.astype(BF16), bmat[j])

    def recur(q):
        for j in range(SLABS):
            lanes = slice(j * SLAB_STATE, (j + 1) * SLAB_STATE)
            a_re, a_im = are[:, lanes], aim[:, lanes]
            h_re, h_im = st_re[:, lanes], st_im[:, lanes]
            for t in range(0, SCAN_STEPS, 2):
                pair = []
                for r in (slice(t * nb, (t + 1) * nb), slice((t + 1) * nb, (t + 2) * nb)):
                    n_re = a_re * h_re - a_im * h_im + bu[q % 2, j, r, 0:SLAB_STATE]
                    n_im = a_re * h_im + a_im * h_re + bu[q % 2, j, r, SLAB_STATE:2 * SLAB_STATE]
                    h_re, h_im = n_re, n_im
                    pair.append(jnp.concatenate([n_re, n_im], axis=1))
                hbuf[q % 2, j, t * nb:(t + 2) * nb, :] = jnp.concatenate(pair, axis=0).astype(BF16)
            st_re[:, lanes] = h_re
            st_im[:, lanes] = h_im

    def project_out(q):
        for j in range(SLABS):
            r = slice(q * sub_rows, (q + 1) * sub_rows)
            d_j = dsk[:, j * LANES:(j + 1) * LANES]
            ys_tm[j, r, :] = _dot(hbuf[q % 2, j], cmat[j]) + d_j * s_tm[j, r, :]

    n_sub = rows // sub_rows
    project_in(0)
    for q in range(n_sub):
        if q + 1 < n_sub:
            project_in(q + 1)
        recur(q)
        if q >= 1:
            project_out(q - 1)
    project_out(n_sub - 1)

    def phase3(blk, carry):
        r0 = pl.multiple_of(blk * ROW_BLOCK, ROW_BLOCK)
        ys = jnp.concatenate(
            [jnp.concatenate([ys_tm[j, pl.ds(blk * per_block + k, CHUNK, stride=nb), :]
                              for j in range(SLABS)], axis=1)
             for k in range(per_block)], axis=0)
        pb = _dot(ys.astype(BF16), w_b[...])
        yb = pb[:, :D_MODEL] * jax.nn.sigmoid(pb[:, D_MODEL:])
        hn = hn_s[pl.ds(r0, ROW_BLOCK), :]
        ga = jax.nn.sigmoid(_dot(hn, w_in[:, OFF_GA:OFF_GA + D_MODEL]))
        gb = jax.nn.sigmoid(_dot(hn, w_in[:, OFF_GB:OFF_GB + D_MODEL]))
        merged = (ga * ya_s[pl.ds(r0, ROW_BLOCK), :] + gb * yb).astype(BF16)
        y = _dot(merged, w_out[...])
        for k in range(per_block):
            b = blk * per_block + k
            x1_ref[b] = x_ref[b] + y[k * CHUNK:(k + 1) * CHUNK]
        return carry

    lax.fori_loop(0, rows // ROW_BLOCK, phase3, 0, unroll=2)

    @pl.when(i == pl.num_programs(0) - 1)
    def _():
        hre_o[...] = st_re[...]
        him_o[...] = st_im[...]


def _mixer(x, xs, h0re, h0im, gmix, w_in, lng, lnb, wsp, bsp, spw, are, aim, bmat, cmat, dsk,
           w_a, w_b, w_out):
    nb, seq, d = x.shape
    ns = xs.shape[0]
    assert ns == DEC_ROWS and 7 * DEC_ROWS <= nb * CHUNK
    rows = nb * CHUNK
    x_spec = pl.BlockSpec((nb, CHUNK, d), lambda i: (0, i, 0))
    st_spec = pl.BlockSpec((nb, STATE_W), lambda i: (0, 0))
    hbm = pl.BlockSpec(memory_space=pl.ANY)
    vmem = lambda a: _const_spec(a.shape)
    return pl.pallas_call(
        _mixer_kernel,
        grid=(seq // CHUNK,),
        in_specs=[x_spec, hbm, hbm, hbm, vmem(gmix), hbm, vmem(lng), vmem(lnb), vmem(wsp),
                  vmem(bsp), vmem(spw), vmem(are), vmem(aim), vmem(bmat), vmem(cmat), vmem(dsk),
                  hbm, hbm, hbm],
        out_specs=(x_spec, st_spec, st_spec, hbm, hbm, hbm, hbm),
        out_shape=(jax.ShapeDtypeStruct(x.shape, F32),
                   jax.ShapeDtypeStruct((nb, STATE_W), F32),
                   jax.ShapeDtypeStruct((nb, STATE_W), F32),
                   jax.ShapeDtypeStruct((ns, d), F32),
                   jax.ShapeDtypeStruct((ns, STATE_W), F32),
                   jax.ShapeDtypeStruct((ns, STATE_W), F32),
                   jax.ShapeDtypeStruct((ns, 1, d), F32)),
        scratch_shapes=[
            pltpu.VMEM((SLABS, rows, LANES), F32),
            pltpu.VMEM((SLABS, rows, LANES), F32),
            pltpu.VMEM((rows, d), F32),
            pltpu.VMEM((rows, d), BF16),
            pltpu.VMEM((2, SLABS, SCAN_STEPS * nb, 2 * SLAB_STATE), F32),
            pltpu.VMEM((2, SLABS, SCAN_STEPS * nb, 2 * SLAB_STATE), BF16),
            pltpu.VMEM((nb, STATE_W), F32),
            pltpu.VMEM((nb, STATE_W), F32),
            pltpu.VMEM(w_in.shape, BF16), pltpu.VMEM(w_a.shape, BF16),
            pltpu.VMEM(w_b.shape, BF16), pltpu.VMEM(w_out.shape, BF16),
            pltpu.SemaphoreType.DMA((STAGE_SLOTS,)),
            pltpu.SemaphoreType.DMA((DEC_IO_COPIES,)),
        ],
        compiler_params=pltpu.CompilerParams(dimension_semantics=("arbitrary",),
                                             vmem_limit_bytes=VMEM_LIMIT),
        name="mixer",
    )(x, xs, h0re, h0im, gmix, w_in, lng, lnb, wsp, bsp, spw, are, aim, bmat, cmat, dsk,
      w_a, w_b, w_out)


def _ffn_rows(x, gffn, w_gate, w_up, w_down, gfin):
    h = _rmsnorm(x, gffn[...]).astype(BF16)
    act = (jax.nn.silu(_dot(h, w_gate[...])) * _dot(h, w_up[...])).astype(BF16)
    x2 = x + _dot(act, w_down[...])
    return _rmsnorm(x2, gfin[...])


def _ffn_kernel(x_ref, xs_ref, gffn, wg_hbm, wu_hbm, wd_hbm, gfin, y_ref, ys_ref,
                w_gate, w_up, w_down, stage, sems):
    @pl.when(pl.program_id(0) == 0)
    def _():
        pieces = (_cast_pieces(wg_hbm, w_gate) + _cast_pieces(wu_hbm, w_up)
                  + _cast_pieces(wd_hbm, w_down))
        _load_cast(pieces, lambda slot, width: stage.at[slot, :, 0:width], sems)

    params = (gffn, w_gate, w_up, w_down, gfin)
    for r0 in range(0, x_ref.shape[0], FFN_SUB_ROWS):
        r = slice(r0, r0 + FFN_SUB_ROWS)
        y_ref[r, :] = _ffn_rows(x_ref[r, :], *params)

    @pl.when(pl.program_id(0) == pl.num_programs(0) - 1)
    def _():
        ys_ref[:, 0, :] = _ffn_rows(xs_ref[...], *params)


def _ffn(x, xs, gffn, w_gate, w_up, w_down, gfin, block_rows):
    n, d = x.shape
    ns = xs.shape[0]
    row_spec = pl.BlockSpec((block_rows, d), lambda i: (i, 0))
    hbm_spec = pl.BlockSpec(memory_space=pl.ANY)
    return pl.pallas_call(
        _ffn_kernel,
        grid=(n // block_rows,),
        in_specs=[row_spec, _const_spec(xs.shape), _const_spec(gffn.shape), hbm_spec, hbm_spec,
                  hbm_spec, _const_spec(gfin.shape)],
        out_specs=(row_spec, pl.BlockSpec((ns, 1, d), lambda i: (0, 0, 0))),
        out_shape=(jax.ShapeDtypeStruct((n, d), F32), jax.ShapeDtypeStruct((ns, 1, d), F32)),
        scratch_shapes=[pltpu.VMEM(w_gate.shape, BF16), pltpu.VMEM(w_up.shape, BF16),
                        pltpu.VMEM(w_down.shape, BF16),
                        pltpu.VMEM((STAGE_SLOTS, STAGE_ROWS, STAGE_COLS), F32),
                        pltpu.SemaphoreType.DMA((STAGE_SLOTS,))],
        compiler_params=pltpu.CompilerParams(dimension_semantics=("arbitrary",),
                                             vmem_limit_bytes=VMEM_LIMIT),
        name="ffn",
    )(x, xs, gffn, w_gate, w_up, w_down, gfin)


def kernel(x_prompt, x_sample, state_ssm_re, state_ssm_im, norm_mix_g, w_in, ln_v_g, ln_v_b,
           w_spatial, b_spatial, ssm_lam_re, ssm_lam_im, ssm_log_dt, ssm_b_re, ssm_b_im,
           ssm_c_re, ssm_c_im, ssm_d, w_branch_a, w_branch_b, w_out, norm_ffn_g,
           w_gate_ffn, w_up_ffn, w_down_ffn, norm_final_g):
    depth = w_in.shape[0]
    assert depth == 1
    nb, seq, d = x_prompt.shape
    ns = x_sample.shape[0]
    row = lambda a: a.reshape(1, -1)

    xp = x_prompt
    xs = x_sample
    gfin = row(norm_final_g)
    p_re, p_im, s_re, s_im, s_v = [], [], [], [], []
    for l in range(depth):
        are, aim, bmat, cmat, wsp, bsp, spw = _prep(
            ssm_lam_re[l], ssm_lam_im[l], ssm_log_dt[l], ssm_b_re[l], ssm_b_im[l],
            ssm_c_re[l], ssm_c_im[l], w_spatial[l], b_spatial[l])
        dsk = row(ssm_d[l])
        ffn_w = (row(norm_ffn_g[l]), w_gate_ffn[l], w_up_ffn[l], w_down_ffn[l])

        xp, hre, him, xs, hs_re, hs_im, v_rows = _mixer(
            xp, xs, state_ssm_re[l].reshape(ns, STATE_W), state_ssm_im[l].reshape(ns, STATE_W),
            row(norm_mix_g[l]), w_in[l], row(ln_v_g[l]), row(ln_v_b[l]), wsp, bsp, spw,
            are, aim, bmat, cmat, dsk, w_branch_a[l], w_branch_b[l], w_out[l])
        p_re.append(hre.reshape(nb, SSM_GROUPS, SSM_STATE))
        p_im.append(him.reshape(nb, SSM_GROUPS, SSM_STATE))
        s_re.append(hs_re.reshape(ns, SSM_GROUPS, SSM_STATE))
        s_im.append(hs_im.reshape(ns, SSM_GROUPS, SSM_STATE))
        s_v.append(v_rows)

        xp, xs = _ffn(xp.reshape(nb * seq, d), xs, *ffn_w, gfin, FFN_BLOCK_ROWS)
        xp = xp.reshape(nb, seq, d)

    return (xp, xs, jnp.stack(p_re), jnp.stack(p_im), jnp.stack(s_re),
            jnp.stack(s_im), jnp.stack(s_v))
```

```python
import jax
import jax.numpy as jnp
from jax import lax
from jax.experimental import pallas as pl
from jax.experimental.pallas import tpu as pltpu

F32 = jnp.float32
BF16 = jnp.bfloat16

D_MODEL = 1024
CHUNK = 128
GMLP_GROUPS = 8
GROUP_CH = D_MODEL // GMLP_GROUPS
SSM_WIDTH = 512
SSM_GROUPS = 32
SSM_STATE = 64
SSM_GROUP_CH = 16
LANES = 128
SUBLANES = 8
SLABS = SSM_WIDTH // LANES
SLAB_GROUPS = LANES // SSM_GROUP_CH
SLAB_STATE = SLAB_GROUPS * SSM_STATE
STATE_W = SSM_GROUPS * SSM_STATE
EPS = 1e-6

OFF_U, OFF_V, OFF_S = 0, D_MODEL, 2 * D_MODEL
OFF_GA = 2 * D_MODEL + SSM_WIDTH
OFF_GB = OFF_GA + D_MODEL

ROW_BLOCK = 256
SCAN_STEPS = 16
FFN_BLOCK_ROWS = 1024
FFN_SUB_ROWS = 256
VMEM_LIMIT = 56 * 1024 * 1024


def _dot(a, b):
    return jnp.dot(a, b, preferred_element_type=F32)


def _rmsnorm(x, g):
    return x * lax.rsqrt(jnp.mean(x * x, axis=-1, keepdims=True) + EPS) * g


def _layernorm(x, g, b):
    mu = jnp.mean(x, axis=-1, keepdims=True)
    xc = x - mu
    var = jnp.mean(xc * xc, axis=-1, keepdims=True)
    return xc * lax.rsqrt(var + EPS) * g + b


def _const_spec(shape):
    zeros = (0,) * len(shape)
    return pl.BlockSpec(shape, lambda *_: zeros, pipeline_mode=pl.Buffered(1))


STAGE_SLOTS = 4
STAGE_ROWS, STAGE_COLS = 256, 1024


def _cast_pieces(src_hbm, dst_vmem):
    n_rows, n_cols = src_hbm.shape
    assert n_rows % STAGE_ROWS == 0 and n_cols % LANES == 0
    pieces = []
    for r in range(0, n_rows, STAGE_ROWS):
        for c in range(0, n_cols, STAGE_COLS):
            w = min(STAGE_COLS, n_cols - c)
            pieces.append((src_hbm.at[r:r + STAGE_ROWS, c:c + w],
                           dst_vmem.at[r:r + STAGE_ROWS, c:c + w], w))
    return pieces


def _load_cast(pieces, stage_view, sems):
    def copy(k):
        src, _, width = pieces[k]
        return pltpu.make_async_copy(src, stage_view(k % STAGE_SLOTS, width), sems.at[k % STAGE_SLOTS])

    for k in range(min(STAGE_SLOTS, len(pieces))):
        copy(k).start()
    for k, (_, dst, width) in enumerate(pieces):
        copy(k).wait()
        dst[...] = stage_view(k % STAGE_SLOTS, width)[...].astype(BF16)
        if k + STAGE_SLOTS < len(pieces):
            copy(k + STAGE_SLOTS).start()


def _dot_exact(a, b):
    return jnp.dot(a, b, preferred_element_type=F32, precision=lax.Precision.HIGHEST)


def _group_mask(shape, row_div, lane_div):
    rows = lax.broadcasted_iota(jnp.int32, shape, 0) // row_div
    lanes = lax.broadcasted_iota(jnp.int32, shape, 1) // lane_div
    return rows == lanes


def _prep_kernel(lre, lim, ldt, bre, bim, cre_in, cim_in, wsp_in, bsp_in,
                 are_o, aim_o, bmat_o, cmat_o, wsp_o, bsp_o, spw_o):
    def slab_rows(row_of_group):
        return jnp.concatenate(
            [jnp.concatenate([row_of_group(j * SLAB_GROUPS + k) for k in range(SLAB_GROUPS)], axis=1)
             for j in range(SLABS)], axis=0)

    lr = slab_rows(lambda g: lre[g:g + 1, :])
    li = slab_rows(lambda g: lim[g:g + 1, :])
    dt = jnp.exp(slab_rows(lambda g: jnp.broadcast_to(ldt[0:1, g:g + 1], (1, SSM_STATE))))
    mag = jnp.exp(lr * dt)
    a_re, a_im = mag * jnp.cos(li * dt), mag * jnp.sin(li * dt)
    n_re, n_im = a_re - 1.0, a_im
    den = lr * lr + li * li
    k_re = (n_re * lr + n_im * li) / den
    k_im = (n_im * lr - n_re * li) / den

    rep_h = (lax.broadcasted_iota(jnp.int32, (SSM_GROUP_CH, LANES), 1) % SSM_GROUP_CH
             == lax.broadcasted_iota(jnp.int32, (SSM_GROUP_CH, LANES), 0)).astype(F32)
    rep_p = (lax.broadcasted_iota(jnp.int32, (SSM_STATE, SLAB_STATE), 1) % SSM_STATE
             == lax.broadcasted_iota(jnp.int32, (SSM_STATE, SLAB_STATE), 0)).astype(F32)
    tb_re = _dot_exact(bre[...], rep_h)
    tb_im = _dot_exact(bim[...], rep_h)
    tc_re = _dot_exact(cre_in[...], rep_p)
    tc_im = _dot_exact(cim_in[...], rep_p)
    mask_b = _group_mask((SLAB_STATE, LANES), SSM_STATE, SSM_GROUP_CH)
    mask_c = _group_mask((LANES, SLAB_STATE), SSM_GROUP_CH, SSM_STATE)

    for j in range(SLABS):
        lanes = slice(j * SLAB_STATE, (j + 1) * SLAB_STATE)
        are_o[:, lanes] = jnp.broadcast_to(a_re[j:j + 1, :], (SUBLANES, SLAB_STATE))
        aim_o[:, lanes] = jnp.broadcast_to(a_im[j:j + 1, :], (SUBLANES, SLAB_STATE))
        rows = slice(j * SLAB_STATE, (j + 1) * SLAB_STATE)
        m_re = jnp.where(mask_b, tb_re[rows], 0.0).T
        m_im = jnp.where(mask_b, tb_im[rows], 0.0).T
        c_re, c_im = k_re[j:j + 1, :], k_im[j:j + 1, :]
        bmat_o[j] = jnp.concatenate([c_re * m_re - c_im * m_im, c_re * m_im + c_im * m_re],
                                    axis=1).astype(BF16)
        rows = slice(j * LANES, (j + 1) * LANES)
        read = jnp.concatenate([jnp.where(mask_c, tc_re[rows], 0.0),
                                jnp.where(mask_c, -tc_im[rows], 0.0)], axis=1)
        cmat_o[j] = read.T.astype(BF16)

    tril = (lax.broadcasted_iota(jnp.int32, (CHUNK, CHUNK), 0)
            >= lax.broadcasted_iota(jnp.int32, (CHUNK, CHUNK), 1))
    spw = []
    for g in range(GMLP_GROUPS):
        w = wsp_in[g]
        wsp_o[g] = jnp.where(tril, w, 0.0).astype(BF16)
        spw.append(jnp.broadcast_to(w[0:1, 0:1], (1, GROUP_CH)))
    spw_o[...] = jnp.concatenate(spw, axis=1)
    sel = _group_mask((D_MODEL, GMLP_GROUPS), GROUP_CH, 1).astype(F32)
    bsp_o[...] = _dot_exact(sel, bsp_in[...]).T


def _prep(lam_re, lam_im, log_dt, b_re, b_im, c_re, c_im, w_spatial, b_spatial):
    g, p, h = b_re.shape
    return pl.pallas_call(
        _prep_kernel,
        out_shape=(jax.ShapeDtypeStruct((SUBLANES, STATE_W), F32),
                   jax.ShapeDtypeStruct((SUBLANES, STATE_W), F32),
                   jax.ShapeDtypeStruct((SLABS, LANES, 2 * SLAB_STATE), BF16),
                   jax.ShapeDtypeStruct((SLABS, 2 * SLAB_STATE, LANES), BF16),
                   jax.ShapeDtypeStruct(w_spatial.shape, BF16),
                   jax.ShapeDtypeStruct((CHUNK, D_MODEL), F32),
                   jax.ShapeDtypeStruct((1, D_MODEL), F32)),
        name="prep",
    )(lam_re, lam_im, log_dt.reshape(1, g), b_re.reshape(g * p, h), b_im.reshape(g * p, h),
      c_re.reshape(g * h, p), c_im.reshape(g * h, p), w_spatial, b_spatial)


def _mixer_prompt_kernel(x_ref, gmix, w_in, lng, lnb, wsp, bsp, are, aim, bmat, cmat, dsk,
                         w_a, w_b, w_out,
                         x1_ref, hre_o, him_o,
                         s_tm, ys_tm, ya_s, hn_s, bu, hbuf, st_re, st_im):
    i = pl.program_id(0)
    nb = x_ref.shape[0]
    rows = nb * CHUNK
    per_block = ROW_BLOCK // CHUNK

    @pl.when(i == 0)
    def _():
        st_re[...] = jnp.zeros_like(st_re)
        st_im[...] = jnp.zeros_like(st_im)

    def project(blk):
        r = slice(blk * ROW_BLOCK, (blk + 1) * ROW_BLOCK)
        x = jnp.concatenate([x_ref[blk * per_block + k] for k in range(per_block)], axis=0)
        hn = _rmsnorm(x, gmix[...]).astype(BF16)
        hn_s[r, :] = hn
        zu = _dot(hn, w_in[:, OFF_U:OFF_U + D_MODEL])
        zv = _dot(hn, w_in[:, OFF_V:OFF_V + D_MODEL])
        s = _dot(hn, w_in[:, OFF_S:OFF_S + SSM_WIDTH])
        for k in range(per_block):
            b = blk * per_block + k
            for j in range(SLABS):
                s_tm[j, pl.ds(b, CHUNK, stride=nb), :] = s[k * CHUNK:(k + 1) * CHUNK,
                                                          j * LANES:(j + 1) * LANES]
        return zu, zv

    def gate(blk, zu, zv):
        u = jax.nn.gelu(zu)
        vb = _layernorm(jax.nn.gelu(zv), lng[...], lnb[...]).astype(BF16)
        mixed = []
        for k in range(per_block):
            cols = [_dot(wsp[g], vb[k * CHUNK:(k + 1) * CHUNK, g * GROUP_CH:(g + 1) * GROUP_CH])
                    for g in range(GMLP_GROUPS)]
            mixed.append(jnp.concatenate(cols, axis=1) + bsp[...])
        gated = (u * jnp.concatenate(mixed, axis=0)).astype(BF16)
        ya_s[blk * ROW_BLOCK:(blk + 1) * ROW_BLOCK, :] = _dot(gated, w_a[...])

    n_blk = rows // ROW_BLOCK
    z = project(0)
    for blk in range(n_blk):
        z_next = project(blk + 1) if blk + 1 < n_blk else None
        gate(blk, *z)
        z = z_next

    sub_rows = SCAN_STEPS * nb

    def project_in(q):
        for j in range(SLABS):
            s_blk = s_tm[j, q * sub_rows:(q + 1) * sub_rows, :]
            bu[q % 2, j] = _dot(s_blk.astype(BF16), bmat[j])

    def recur(q):
        for j in range(SLABS):
            lanes = slice(j * SLAB_STATE, (j + 1) * SLAB_STATE)
            a_re, a_im = are[:, lanes], aim[:, lanes]
            h_re, h_im = st_re[:, lanes], st_im[:, lanes]
            for t in range(0, SCAN_STEPS, 2):
                pair = []
                for r in (slice(t * nb, (t + 1) * nb), slice((t + 1) * nb, (t + 2) * nb)):
                    n_re = a_re * h_re - a_im * h_im + bu[q % 2, j, r, 0:SLAB_STATE]
                    n_im = a_re * h_im + a_im * h_re + bu[q % 2, j, r, SLAB_STATE:2 * SLAB_STATE]
                    h_re, h_im = n_re, n_im
                    pair.append(jnp.concatenate([n_re, n_im], axis=1))
                hbuf[q % 2, j, t * nb:(t + 2) * nb, :] = jnp.concatenate(pair, axis=0).astype(BF16)
            st_re[:, lanes] = h_re
            st_im[:, lanes] = h_im

    def project_out(q):
        for j in range(SLABS):
            r = slice(q * sub_rows, (q + 1) * sub_rows)
            d_j = dsk[:, j * LANES:(j + 1) * LANES]
            ys_tm[j, r, :] = _dot(hbuf[q % 2, j], cmat[j]) + d_j * s_tm[j, r, :]

    n_sub = rows // sub_rows
    project_in(0)
    for q in range(n_sub):
        if q + 1 < n_sub:
            project_in(q + 1)
        recur(q)
        if q >= 1:
            project_out(q - 1)
    project_out(n_sub - 1)

    def phase3(blk, carry):
        r0 = pl.multiple_of(blk * ROW_BLOCK, ROW_BLOCK)
        ys = jnp.concatenate(
            [jnp.concatenate([ys_tm[j, pl.ds(blk * per_block + k, CHUNK, stride=nb), :]
                              for j in range(SLABS)], axis=1)
             for k in range(per_block)], axis=0)
        pb = _dot(ys.astype(BF16), w_b[...])
        yb = pb[:, :D_MODEL] * jax.nn.sigmoid(pb[:, D_MODEL:])
        hn = hn_s[pl.ds(r0, ROW_BLOCK), :]
        ga = jax.nn.sigmoid(_dot(hn, w_in[:, OFF_GA:OFF_GA + D_MODEL]))
        gb = jax.nn.sigmoid(_dot(hn, w_in[:, OFF_GB:OFF_GB + D_MODEL]))
        merged = (ga * ya_s[pl.ds(r0, ROW_BLOCK), :] + gb * yb).astype(BF16)
        y = _dot(merged, w_out[...])
        for k in range(per_block):
            b = blk * per_block + k
            x1_ref[b] = x_ref[b] + y[k * CHUNK:(k + 1) * CHUNK]
        return carry

    lax.fori_loop(0, rows // ROW_BLOCK, phase3, 0, unroll=2)

    @pl.when(i == pl.num_programs(0) - 1)
    def _():
        hre_o[...] = st_re[...]
        him_o[...] = st_im[...]


def _mixer_prompt(x, gmix, w_in, lng, lnb, wsp, bsp, are, aim, bmat, cmat, dsk, w_a, w_b, w_out):
    nb, seq, d = x.shape
    rows = nb * CHUNK
    consts = (gmix, w_in, lng, lnb, wsp, bsp, are, aim, bmat, cmat, dsk, w_a, w_b, w_out)
    x_spec = pl.BlockSpec((nb, CHUNK, d), lambda i: (0, i, 0))
    st_spec = pl.BlockSpec((nb, STATE_W), lambda i: (0, 0))
    return pl.pallas_call(
        _mixer_prompt_kernel,
        grid=(seq // CHUNK,),
        in_specs=[x_spec] + [_const_spec(c.shape) for c in consts],
        out_specs=(x_spec, st_spec, st_spec),
        out_shape=(jax.ShapeDtypeStruct(x.shape, F32),
                   jax.ShapeDtypeStruct((nb, STATE_W), F32),
                   jax.ShapeDtypeStruct((nb, STATE_W), F32)),
        scratch_shapes=[
            pltpu.VMEM((SLABS, rows, LANES), F32),
            pltpu.VMEM((SLABS, rows, LANES), F32),
            pltpu.VMEM((rows, d), F32),
            pltpu.VMEM((rows, d), BF16),
            pltpu.VMEM((2, SLABS, SCAN_STEPS * nb, 2 * SLAB_STATE), F32),
            pltpu.VMEM((2, SLABS, SCAN_STEPS * nb, 2 * SLAB_STATE), BF16),
            pltpu.VMEM((nb, STATE_W), F32),
            pltpu.VMEM((nb, STATE_W), F32),
        ],
        compiler_params=pltpu.CompilerParams(dimension_semantics=("arbitrary",),
                                             vmem_limit_bytes=VMEM_LIMIT),
        name="mixer_prompt",
    )(x, *consts)


def _mixer_sample_kernel(x_ref, h0re, h0im, gmix, w_in, lng, lnb, spw, bsp, are, aim, bmat, cmat,
                         dsk, w_a, w_b, w_out,
                         x1_ref, hre_o, him_o, v_o):
    x = x_ref[:, 0, :]
    hn = _rmsnorm(x, gmix[...]).astype(BF16)
    u = jax.nn.gelu(_dot(hn, w_in[:, OFF_U:OFF_U + D_MODEL]))
    v = _layernorm(jax.nn.gelu(_dot(hn, w_in[:, OFF_V:OFF_V + D_MODEL])), lng[...], lnb[...])
    v_o[:, 0, :] = v
    mixed = v * spw[...] + bsp[0:1, :]
    ya = _dot((u * mixed).astype(BF16), w_a[...])
    s = _dot(hn, w_in[:, OFF_S:OFF_S + SSM_WIDTH])
    ys = []
    for j in range(SLABS):
        lanes = slice(j * SLAB_STATE, (j + 1) * SLAB_STATE)
        s_j = s[:, j * LANES:(j + 1) * LANES]
        bu = _dot(s_j.astype(BF16), bmat[j])
        a_re, a_im = are[0:1, lanes], aim[0:1, lanes]
        p_re, p_im = h0re[:, lanes], h0im[:, lanes]
        n_re = a_re * p_re - a_im * p_im + bu[:, 0:SLAB_STATE]
        n_im = a_re * p_im + a_im * p_re + bu[:, SLAB_STATE:]
        hre_o[:, lanes] = n_re
        him_o[:, lanes] = n_im
        h = jnp.concatenate([n_re, n_im], axis=1).astype(BF16)
        ys.append(_dot(h, cmat[j]) + dsk[:, j * LANES:(j + 1) * LANES] * s_j)
    pb = _dot(jnp.concatenate(ys, axis=1).astype(BF16), w_b[...])
    yb = pb[:, :D_MODEL] * jax.nn.sigmoid(pb[:, D_MODEL:])
    ga = jax.nn.sigmoid(_dot(hn, w_in[:, OFF_GA:OFF_GA + D_MODEL]))
    gb = jax.nn.sigmoid(_dot(hn, w_in[:, OFF_GB:OFF_GB + D_MODEL]))
    merged = (ga * ya + gb * yb).astype(BF16)
    x1_ref[...] = x + _dot(merged, w_out[...])


def _mixer_sample(x, h0re, h0im, gmix, w_in, lng, lnb, spw, bsp, are, aim, bmat, cmat, dsk,
                  w_a, w_b, w_out):
    n, _, d = x.shape
    return pl.pallas_call(
        _mixer_sample_kernel,
        out_shape=(jax.ShapeDtypeStruct((n, d), F32),
                   jax.ShapeDtypeStruct((n, STATE_W), F32),
                   jax.ShapeDtypeStruct((n, STATE_W), F32),
                   jax.ShapeDtypeStruct((n, 1, d), F32)),
        compiler_params=pltpu.CompilerParams(vmem_limit_bytes=VMEM_LIMIT),
        name="mixer_sample",
    )(x, h0re, h0im, gmix, w_in, lng, lnb, spw, bsp, are, aim, bmat, cmat, dsk, w_a, w_b, w_out)


def _ffn_rows(x, gffn, w_gate, w_up, w_down, gfin):
    h = _rmsnorm(x, gffn[...]).astype(BF16)
    act = (jax.nn.silu(_dot(h, w_gate[...])) * _dot(h, w_up[...])).astype(BF16)
    x2 = x + _dot(act, w_down[...])
    return _rmsnorm(x2, gfin[...])


def _ffn_kernel(x_ref, xs_ref, gffn, wg_hbm, wu_hbm, wd_hbm, gfin, y_ref, ys_ref,
                w_gate, w_up, w_down, stage, sems):
    @pl.when(pl.program_id(0) == 0)
    def _():
        pieces = (_cast_pieces(wg_hbm, w_gate) + _cast_pieces(wu_hbm, w_up)
                  + _cast_pieces(wd_hbm, w_down))
        _load_cast(pieces, lambda slot, width: stage.at[slot, :, 0:width], sems)

    params = (gffn, w_gate, w_up, w_down, gfin)
    for r0 in range(0, x_ref.shape[0], FFN_SUB_ROWS):
        r = slice(r0, r0 + FFN_SUB_ROWS)
        y_ref[r, :] = _ffn_rows(x_ref[r, :], *params)

    @pl.when(pl.program_id(0) == pl.num_programs(0) - 1)
    def _():
        ys_ref[:, 0, :] = _ffn_rows(xs_ref[...], *params)


def _ffn(x, xs, gffn, w_gate, w_up, w_down, gfin, block_rows):
    n, d = x.shape
    ns = xs.shape[0]
    row_spec = pl.BlockSpec((block_rows, d), lambda i: (i, 0))
    hbm_spec = pl.BlockSpec(memory_space=pl.ANY)
    return pl.pallas_call(
        _ffn_kernel,
        grid=(n // block_rows,),
        in_specs=[row_spec, _const_spec(xs.shape), _const_spec(gffn.shape), hbm_spec, hbm_spec,
                  hbm_spec, _const_spec(gfin.shape)],
        out_specs=(row_spec, pl.BlockSpec((ns, 1, d), lambda i: (0, 0, 0))),
        out_shape=(jax.ShapeDtypeStruct((n, d), F32), jax.ShapeDtypeStruct((ns, 1, d), F32)),
        scratch_shapes=[pltpu.VMEM(w_gate.shape, BF16), pltpu.VMEM(w_up.shape, BF16),
                        pltpu.VMEM(w_down.shape, BF16),
                        pltpu.VMEM((STAGE_SLOTS, STAGE_ROWS, STAGE_COLS), F32),
                        pltpu.SemaphoreType.DMA((STAGE_SLOTS,))],
        compiler_params=pltpu.CompilerParams(dimension_semantics=("arbitrary",),
                                             vmem_limit_bytes=VMEM_LIMIT),
        name="ffn",
    )(x, xs, gffn, w_gate, w_up, w_down, gfin)


def kernel(x_prompt, x_sample, state_ssm_re, state_ssm_im, norm_mix_g, w_in, ln_v_g, ln_v_b,
           w_spatial, b_spatial, ssm_lam_re, ssm_lam_im, ssm_log_dt, ssm_b_re, ssm_b_im,
           ssm_c_re, ssm_c_im, ssm_d, w_branch_a, w_branch_b, w_out, norm_ffn_g,
           w_gate_ffn, w_up_ffn, w_down_ffn, norm_final_g):
    depth = w_in.shape[0]
    assert depth == 1
    nb, seq, d = x_prompt.shape
    ns = x_sample.shape[0]
    row = lambda a: a.reshape(1, -1)

    xp = x_prompt
    xs = x_sample
    gfin = row(norm_final_g)
    p_re, p_im, s_re, s_im, s_v = [], [], [], [], []
    for l in range(depth):
        are, aim, bmat, cmat, wsp, bsp, spw = _prep(
            ssm_lam_re[l], ssm_lam_im[l], ssm_log_dt[l], ssm_b_re[l], ssm_b_im[l],
            ssm_c_re[l], ssm_c_im[l], w_spatial[l], b_spatial[l])
        dsk = row(ssm_d[l])
        shared = dict(gmix=row(norm_mix_g[l]), w_in=w_in[l].astype(BF16), lng=row(ln_v_g[l]),
                      lnb=row(ln_v_b[l]))
        tail = dict(are=are, aim=aim, bmat=bmat, cmat=cmat, dsk=dsk,
                    w_a=w_branch_a[l].astype(BF16), w_b=w_branch_b[l].astype(BF16),
                    w_out=w_out[l].astype(BF16))
        ffn_w = (row(norm_ffn_g[l]), w_gate_ffn[l], w_up_ffn[l], w_down_ffn[l])

        xp, hre, him = _mixer_prompt(xp, **shared, wsp=wsp, bsp=bsp, **tail)
        p_re.append(hre.reshape(nb, SSM_GROUPS, SSM_STATE))
        p_im.append(him.reshape(nb, SSM_GROUPS, SSM_STATE))

        xs, hre, him, v_rows = _mixer_sample(
            xs, state_ssm_re[l].reshape(ns, STATE_W), state_ssm_im[l].reshape(ns, STATE_W),
            **shared, spw=spw, bsp=bsp, **tail)
        s_re.append(hre.reshape(ns, SSM_GROUPS, SSM_STATE))
        s_im.append(him.reshape(ns, SSM_GROUPS, SSM_STATE))
        s_v.append(v_rows)

        xp, xs = _ffn(xp.reshape(nb * seq, d), xs, *ffn_w, gfin, FFN_BLOCK_ROWS)
        xp = xp.reshape(nb, seq, d)

    return (xp, xs, jnp.stack(p_re), jnp.stack(p_im), jnp.stack(s_re),
            jnp.stack(s_im), jnp.stack(s_v))
```

```python
import jax
import jax.numpy as jnp
from jax import lax
from jax.experimental import pallas as pl
from jax.experimental.pallas import tpu as pltpu

F32 = jnp.float32
BF16 = jnp.bfloat16

D_MODEL = 1024
CHUNK = 128
GMLP_GROUPS = 8
GROUP_CH = D_MODEL // GMLP_GROUPS
SSM_WIDTH = 512
SSM_GROUPS = 32
SSM_STATE = 64
SSM_GROUP_CH = 16
LANES = 128
SUBLANES = 8
SLABS = SSM_WIDTH // LANES
SLAB_GROUPS = LANES // SSM_GROUP_CH
SLAB_STATE = SLAB_GROUPS * SSM_STATE
STATE_W = SSM_GROUPS * SSM_STATE
EPS = 1e-6

OFF_U, OFF_V, OFF_S = 0, D_MODEL, 2 * D_MODEL
OFF_GA = 2 * D_MODEL + SSM_WIDTH
OFF_GB = OFF_GA + D_MODEL

ROW_BLOCK = 256
SCAN_STEPS = 16
FFN_BLOCK_ROWS = 1024
FFN_SUB_ROWS = 256
VMEM_LIMIT = 56 * 1024 * 1024


def _dot(a, b):
    return jnp.dot(a, b, preferred_element_type=F32)


def _rmsnorm(x, g):
    return x * lax.rsqrt(jnp.mean(x * x, axis=-1, keepdims=True) + EPS) * g


def _layernorm(x, g, b):
    mu = jnp.mean(x, axis=-1, keepdims=True)
    xc = x - mu
    var = jnp.mean(xc * xc, axis=-1, keepdims=True)
    return xc * lax.rsqrt(var + EPS) * g + b


def _const_spec(shape):
    zeros = (0,) * len(shape)
    return pl.BlockSpec(shape, lambda *_: zeros, pipeline_mode=pl.Buffered(1))


STAGE_SLOTS = 8
STAGE_ROWS, STAGE_COLS = 256, 1024


def _cast_pieces(src_hbm, dst_vmem):
    n_rows, n_cols = src_hbm.shape
    assert n_rows % STAGE_ROWS == 0 and n_cols % LANES == 0
    pieces = []
    for r in range(0, n_rows, STAGE_ROWS):
        for c in range(0, n_cols, STAGE_COLS):
            w = min(STAGE_COLS, n_cols - c)
            pieces.append((src_hbm.at[r:r + STAGE_ROWS, c:c + w],
                           dst_vmem.at[r:r + STAGE_ROWS, c:c + w], w))
    return pieces


def _load_cast(pieces, stage_view, sems):
    def copy(k):
        src, _, width = pieces[k]
        return pltpu.make_async_copy(src, stage_view(k % STAGE_SLOTS, width), sems.at[k % STAGE_SLOTS])

    for k in range(min(STAGE_SLOTS, len(pieces))):
        copy(k).start()
    for k, (_, dst, width) in enumerate(pieces):
        copy(k).wait()
        dst[...] = stage_view(k % STAGE_SLOTS, width)[...].astype(BF16)
        if k + STAGE_SLOTS < len(pieces):
            copy(k + STAGE_SLOTS).start()


def _dot_exact(a, b):
    return jnp.dot(a, b, preferred_element_type=F32, precision=lax.Precision.HIGHEST)


def _group_mask(shape, row_div, lane_div):
    rows = lax.broadcasted_iota(jnp.int32, shape, 0) // row_div
    lanes = lax.broadcasted_iota(jnp.int32, shape, 1) // lane_div
    return rows == lanes


def _prep_kernel(lre, lim, ldt, bre, bim, cre_in, cim_in, wsp_in, bsp_in,
                 are_o, aim_o, bmat_o, cmat_o, wsp_o, bsp_o, spw_o):
    def slab_rows(row_of_group):
        return jnp.concatenate(
            [jnp.concatenate([row_of_group(j * SLAB_GROUPS + k) for k in range(SLAB_GROUPS)], axis=1)
             for j in range(SLABS)], axis=0)

    lr = slab_rows(lambda g: lre[g:g + 1, :])
    li = slab_rows(lambda g: lim[g:g + 1, :])
    dt = jnp.exp(slab_rows(lambda g: jnp.broadcast_to(ldt[0:1, g:g + 1], (1, SSM_STATE))))
    mag = jnp.exp(lr * dt)
    a_re, a_im = mag * jnp.cos(li * dt), mag * jnp.sin(li * dt)
    n_re, n_im = a_re - 1.0, a_im
    den = lr * lr + li * li
    k_re = (n_re * lr + n_im * li) / den
    k_im = (n_im * lr - n_re * li) / den

    rep_h = (lax.broadcasted_iota(jnp.int32, (SSM_GROUP_CH, LANES), 1) % SSM_GROUP_CH
             == lax.broadcasted_iota(jnp.int32, (SSM_GROUP_CH, LANES), 0)).astype(F32)
    rep_p = (lax.broadcasted_iota(jnp.int32, (SSM_STATE, SLAB_STATE), 1) % SSM_STATE
             == lax.broadcasted_iota(jnp.int32, (SSM_STATE, SLAB_STATE), 0)).astype(F32)
    tb_re = _dot_exact(bre[...], rep_h)
    tb_im = _dot_exact(bim[...], rep_h)
    tc_re = _dot_exact(cre_in[...], rep_p)
    tc_im = _dot_exact(cim_in[...], rep_p)
    mask_b = _group_mask((SLAB_STATE, LANES), SSM_STATE, SSM_GROUP_CH)
    mask_c = _group_mask((LANES, SLAB_STATE), SSM_GROUP_CH, SSM_STATE)

    for j in range(SLABS):
        lanes = slice(j * SLAB_STATE, (j + 1) * SLAB_STATE)
        are_o[:, lanes] = jnp.broadcast_to(a_re[j:j + 1, :], (SUBLANES, SLAB_STATE))
        aim_o[:, lanes] = jnp.broadcast_to(a_im[j:j + 1, :], (SUBLANES, SLAB_STATE))
        rows = slice(j * SLAB_STATE, (j + 1) * SLAB_STATE)
        m_re = jnp.where(mask_b, tb_re[rows], 0.0).T
        m_im = jnp.where(mask_b, tb_im[rows], 0.0).T
        c_re, c_im = k_re[j:j + 1, :], k_im[j:j + 1, :]
        bmat_o[j] = jnp.concatenate([c_re * m_re - c_im * m_im, c_re * m_im + c_im * m_re],
                                    axis=1).astype(BF16)
        rows = slice(j * LANES, (j + 1) * LANES)
        read = jnp.concatenate([jnp.where(mask_c, tc_re[rows], 0.0),
                                jnp.where(mask_c, -tc_im[rows], 0.0)], axis=1)
        cmat_o[j] = read.T.astype(BF16)

    tril = (lax.broadcasted_iota(jnp.int32, (CHUNK, CHUNK), 0)
            >= lax.broadcasted_iota(jnp.int32, (CHUNK, CHUNK), 1))
    spw = []
    for g in range(GMLP_GROUPS):
        w = wsp_in[g]
        wsp_o[g] = jnp.where(tril, w, 0.0).astype(BF16)
        spw.append(jnp.broadcast_to(w[0:1, 0:1], (1, GROUP_CH)))
    spw_o[...] = jnp.concatenate(spw, axis=1)
    sel = _group_mask((D_MODEL, GMLP_GROUPS), GROUP_CH, 1).astype(F32)
    bsp_o[...] = _dot_exact(sel, bsp_in[...]).T


def _prep(lam_re, lam_im, log_dt, b_re, b_im, c_re, c_im, w_spatial, b_spatial):
    g, p, h = b_re.shape
    return pl.pallas_call(
        _prep_kernel,
        out_shape=(jax.ShapeDtypeStruct((SUBLANES, STATE_W), F32),
                   jax.ShapeDtypeStruct((SUBLANES, STATE_W), F32),
                   jax.ShapeDtypeStruct((SLABS, LANES, 2 * SLAB_STATE), BF16),
                   jax.ShapeDtypeStruct((SLABS, 2 * SLAB_STATE, LANES), BF16),
                   jax.ShapeDtypeStruct(w_spatial.shape, BF16),
                   jax.ShapeDtypeStruct((CHUNK, D_MODEL), F32),
                   jax.ShapeDtypeStruct((1, D_MODEL), F32)),
        name="prep",
    )(lam_re, lam_im, log_dt.reshape(1, g), b_re.reshape(g * p, h), b_im.reshape(g * p, h),
      c_re.reshape(g * h, p), c_im.reshape(g * h, p), w_spatial, b_spatial)


def _mixer_prompt_kernel(x_ref, gmix, w_in, lng, lnb, wsp, bsp, are, aim, bmat, cmat, dsk,
                         w_a, w_b, w_out,
                         x1_ref, hre_o, him_o,
                         s_tm, ys_tm, ya_s, hn_s, bu, hbuf, st_re, st_im):
    i = pl.program_id(0)
    nb = x_ref.shape[0]
    rows = nb * CHUNK
    per_block = ROW_BLOCK // CHUNK

    @pl.when(i == 0)
    def _():
        st_re[...] = jnp.zeros_like(st_re)
        st_im[...] = jnp.zeros_like(st_im)

    def project(blk):
        r = slice(blk * ROW_BLOCK, (blk + 1) * ROW_BLOCK)
        x = jnp.concatenate([x_ref[blk * per_block + k] for k in range(per_block)], axis=0)
        hn = _rmsnorm(x, gmix[...]).astype(BF16)
        hn_s[r, :] = hn
        zu = _dot(hn, w_in[:, OFF_U:OFF_U + D_MODEL])
        zv = _dot(hn, w_in[:, OFF_V:OFF_V + D_MODEL])
        s = _dot(hn, w_in[:, OFF_S:OFF_S + SSM_WIDTH])
        for k in range(per_block):
            b = blk * per_block + k
            for j in range(SLABS):
                s_tm[j, pl.ds(b, CHUNK, stride=nb), :] = s[k * CHUNK:(k + 1) * CHUNK,
                                                          j * LANES:(j + 1) * LANES]
        return zu, zv

    def gate(blk, zu, zv):
        u = jax.nn.gelu(zu)
        vb = _layernorm(jax.nn.gelu(zv), lng[...], lnb[...]).astype(BF16)
        mixed = []
        for k in range(per_block):
            cols = [_dot(wsp[g], vb[k * CHUNK:(k + 1) * CHUNK, g * GROUP_CH:(g + 1) * GROUP_CH])
                    for g in range(GMLP_GROUPS)]
            mixed.append(jnp.concatenate(cols, axis=1) + bsp[...])
        gated = (u * jnp.concatenate(mixed, axis=0)).astype(BF16)
        ya_s[blk * ROW_BLOCK:(blk + 1) * ROW_BLOCK, :] = _dot(gated, w_a[...])

    n_blk = rows // ROW_BLOCK
    z = project(0)
    for blk in range(n_blk):
        z_next = project(blk + 1) if blk + 1 < n_blk else None
        gate(blk, *z)
        z = z_next

    sub_rows = SCAN_STEPS * nb

    def project_in(q):
        for j in range(SLABS):
            s_blk = s_tm[j, q * sub_rows:(q + 1) * sub_rows, :]
            bu[q % 2, j] = _dot(s_blk.astype(BF16), bmat[j])

    def recur(q):
        for j in range(SLABS):
            lanes = slice(j * SLAB_STATE, (j + 1) * SLAB_STATE)
            a_re, a_im = are[:, lanes], aim[:, lanes]
            h_re, h_im = st_re[:, lanes], st_im[:, lanes]
            for t in range(0, SCAN_STEPS, 2):
                pair = []
                for r in (slice(t * nb, (t + 1) * nb), slice((t + 1) * nb, (t + 2) * nb)):
                    n_re = a_re * h_re - a_im * h_im + bu[q % 2, j, r, 0:SLAB_STATE]
                    n_im = a_re * h_im + a_im * h_re + bu[q % 2, j, r, SLAB_STATE:2 * SLAB_STATE]
                    h_re, h_im = n_re, n_im
                    pair.append(jnp.concatenate([n_re, n_im], axis=1))
                hbuf[q % 2, j, t * nb:(t + 2) * nb, :] = jnp.concatenate(pair, axis=0).astype(BF16)
            st_re[:, lanes] = h_re
            st_im[:, lanes] = h_im

    def project_out(q):
        for j in range(SLABS):
            r = slice(q * sub_rows, (q + 1) * sub_rows)
            d_j = dsk[:, j * LANES:(j + 1) * LANES]
            ys_tm[j, r, :] = _dot(hbuf[q % 2, j], cmat[j]) + d_j * s_tm[j, r, :]

    n_sub = rows // sub_rows
    project_in(0)
    for q in range(n_sub):
        if q + 1 < n_sub:
            project_in(q + 1)
        recur(q)
        if q >= 1:
            project_out(q - 1)
    project_out(n_sub - 1)

    def phase3(blk, carry):
        r0 = pl.multiple_of(blk * ROW_BLOCK, ROW_BLOCK)
        ys = jnp.concatenate(
            [jnp.concatenate([ys_tm[j, pl.ds(blk * per_block + k, CHUNK, stride=nb), :]
                              for j in range(SLABS)], axis=1)
             for k in range(per_block)], axis=0)
        pb = _dot(ys.astype(BF16), w_b[...])
        yb = pb[:, :D_MODEL] * jax.nn.sigmoid(pb[:, D_MODEL:])
        hn = hn_s[pl.ds(r0, ROW_BLOCK), :]
        ga = jax.nn.sigmoid(_dot(hn, w_in[:, OFF_GA:OFF_GA + D_MODEL]))
        gb = jax.nn.sigmoid(_dot(hn, w_in[:, OFF_GB:OFF_GB + D_MODEL]))
        merged = (ga * ya_s[pl.ds(r0, ROW_BLOCK), :] + gb * yb).astype(BF16)
        y = _dot(merged, w_out[...])
        for k in range(per_block):
            b = blk * per_block + k
            x1_ref[b] = x_ref[b] + y[k * CHUNK:(k + 1) * CHUNK]
        return carry

    lax.fori_loop(0, rows // ROW_BLOCK, phase3, 0, unroll=2)

    @pl.when(i == pl.num_programs(0) - 1)
    def _():
        hre_o[...] = st_re[...]
        him_o[...] = st_im[...]


def _mixer_prompt(x, gmix, w_in, lng, lnb, wsp, bsp, are, aim, bmat, cmat, dsk, w_a, w_b, w_out):
    nb, seq, d = x.shape
    rows = nb * CHUNK
    consts = (gmix, w_in, lng, lnb, wsp, bsp, are, aim, bmat, cmat, dsk, w_a, w_b, w_out)
    x_spec = pl.BlockSpec((nb, CHUNK, d), lambda i: (0, i, 0))
    st_spec = pl.BlockSpec((nb, STATE_W), lambda i: (0, 0))
    return pl.pallas_call(
        _mixer_prompt_kernel,
        grid=(seq // CHUNK,),
        in_specs=[x_spec] + [_const_spec(c.shape) for c in consts],
        out_specs=(x_spec, st_spec, st_spec),
        out_shape=(jax.ShapeDtypeStruct(x.shape, F32),
                   jax.ShapeDtypeStruct((nb, STATE_W), F32),
                   jax.ShapeDtypeStruct((nb, STATE_W), F32)),
        scratch_shapes=[
            pltpu.VMEM((SLABS, rows, LANES), F32),
            pltpu.VMEM((SLABS, rows, LANES), F32),
            pltpu.VMEM((rows, d), F32),
            pltpu.VMEM((rows, d), BF16),
            pltpu.VMEM((2, SLABS, SCAN_STEPS * nb, 2 * SLAB_STATE), F32),
            pltpu.VMEM((2, SLABS, SCAN_STEPS * nb, 2 * SLAB_STATE), BF16),
            pltpu.VMEM((nb, STATE_W), F32),
            pltpu.VMEM((nb, STATE_W), F32),
        ],
        compiler_params=pltpu.CompilerParams(dimension_semantics=("arbitrary",),
                                             vmem_limit_bytes=VMEM_LIMIT),
        name="mixer_prompt",
    )(x, *consts)


def _mixer_sample_kernel(x_ref, h0re, h0im, gmix, w_in, lng, lnb, spw, bsp, are, aim, bmat, cmat,
                         dsk, w_a, w_b, w_out,
                         x1_ref, hre_o, him_o, v_o):
    x = x_ref[:, 0, :]
    hn = _rmsnorm(x, gmix[...]).astype(BF16)
    u = jax.nn.gelu(_dot(hn, w_in[:, OFF_U:OFF_U + D_MODEL]))
    v = _layernorm(jax.nn.gelu(_dot(hn, w_in[:, OFF_V:OFF_V + D_MODEL])), lng[...], lnb[...])
    v_o[:, 0, :] = v
    mixed = v * spw[...] + bsp[0:1, :]
    ya = _dot((u * mixed).astype(BF16), w_a[...])
    s = _dot(hn, w_in[:, OFF_S:OFF_S + SSM_WIDTH])
    ys = []
    for j in range(SLABS):
        lanes = slice(j * SLAB_STATE, (j + 1) * SLAB_STATE)
        s_j = s[:, j * LANES:(j + 1) * LANES]
        bu = _dot(s_j.astype(BF16), bmat[j])
        a_re, a_im = are[0:1, lanes], aim[0:1, lanes]
        p_re, p_im = h0re[:, lanes], h0im[:, lanes]
        n_re = a_re * p_re - a_im * p_im + bu[:, 0:SLAB_STATE]
        n_im = a_re * p_im + a_im * p_re + bu[:, SLAB_STATE:]
        hre_o[:, lanes] = n_re
        him_o[:, lanes] = n_im
        h = jnp.concatenate([n_re, n_im], axis=1).astype(BF16)
        ys.append(_dot(h, cmat[j]) + dsk[:, j * LANES:(j + 1) * LANES] * s_j)
    pb = _dot(jnp.concatenate(ys, axis=1).astype(BF16), w_b[...])
    yb = pb[:, :D_MODEL] * jax.nn.sigmoid(pb[:, D_MODEL:])
    ga = jax.nn.sigmoid(_dot(hn, w_in[:, OFF_GA:OFF_GA + D_MODEL]))
    gb = jax.nn.sigmoid(_dot(hn, w_in[:, OFF_GB:OFF_GB + D_MODEL]))
    merged = (ga * ya + gb * yb).astype(BF16)
    x1_ref[...] = x + _dot(merged, w_out[...])


def _mixer_sample(x, h0re, h0im, gmix, w_in, lng, lnb, spw, bsp, are, aim, bmat, cmat, dsk,
                  w_a, w_b, w_out):
    n, _, d = x.shape
    return pl.pallas_call(
        _mixer_sample_kernel,
        out_shape=(jax.ShapeDtypeStruct((n, d), F32),
                   jax.ShapeDtypeStruct((n, STATE_W), F32),
                   jax.ShapeDtypeStruct((n, STATE_W), F32),
                   jax.ShapeDtypeStruct((n, 1, d), F32)),
        compiler_params=pltpu.CompilerParams(vmem_limit_bytes=VMEM_LIMIT),
        name="mixer_sample",
    )(x, h0re, h0im, gmix, w_in, lng, lnb, spw, bsp, are, aim, bmat, cmat, dsk, w_a, w_b, w_out)


def _ffn_rows(x, gffn, w_gate, w_up, w_down, gfin):
    h = _rmsnorm(x, gffn[...]).astype(BF16)
    act = (jax.nn.silu(_dot(h, w_gate[...])) * _dot(h, w_up[...])).astype(BF16)
    x2 = x + _dot(act, w_down[...])
    return _rmsnorm(x2, gfin[...])


def _ffn_kernel(x_ref, xs_ref, gffn, wg_hbm, wu_hbm, wd_hbm, gfin, y_ref, ys_ref,
                w_gate, w_up, w_down, stage, sems):
    @pl.when(pl.program_id(0) == 0)
    def _():
        pieces = (_cast_pieces(wg_hbm, w_gate) + _cast_pieces(wu_hbm, w_up)
                  + _cast_pieces(wd_hbm, w_down))
        _load_cast(pieces, lambda slot, width: stage.at[slot, :, 0:width], sems)

    params = (gffn, w_gate, w_up, w_down, gfin)
    for r0 in range(0, x_ref.shape[0], FFN_SUB_ROWS):
        r = slice(r0, r0 + FFN_SUB_ROWS)
        y_ref[r, :] = _ffn_rows(x_ref[r, :], *params)

    @pl.when(pl.program_id(0) == pl.num_programs(0) - 1)
    def _():
        ys_ref[:, 0, :] = _ffn_rows(xs_ref[...], *params)


def _ffn(x, xs, gffn, w_gate, w_up, w_down, gfin, block_rows):
    n, d = x.shape
    ns = xs.shape[0]
    row_spec = pl.BlockSpec((block_rows, d), lambda i: (i, 0))
    hbm_spec = pl.BlockSpec(memory_space=pl.ANY)
    return pl.pallas_call(
        _ffn_kernel,
        grid=(n // block_rows,),
        in_specs=[row_spec, _const_spec(xs.shape), _const_spec(gffn.shape), hbm_spec, hbm_spec,
                  hbm_spec, _const_spec(gfin.shape)],
        out_specs=(row_spec, pl.BlockSpec((ns, 1, d), lambda i: (0, 0, 0))),
        out_shape=(jax.ShapeDtypeStruct((n, d), F32), jax.ShapeDtypeStruct((ns, 1, d), F32)),
        scratch_shapes=[pltpu.VMEM(w_gate.shape, BF16), pltpu.VMEM(w_up.shape, BF16),
                        pltpu.VMEM(w_down.shape, BF16),
                        pltpu.VMEM((STAGE_SLOTS, STAGE_ROWS, STAGE_COLS), F32),
                        pltpu.SemaphoreType.DMA((STAGE_SLOTS,))],
        compiler_params=pltpu.CompilerParams(dimension_semantics=("arbitrary",),
                                             vmem_limit_bytes=VMEM_LIMIT),
        name="ffn",
    )(x, xs, gffn, w_gate, w_up, w_down, gfin)


def kernel(x_prompt, x_sample, state_ssm_re, state_ssm_im, norm_mix_g, w_in, ln_v_g, ln_v_b,
           w_spatial, b_spatial, ssm_lam_re, ssm_lam_im, ssm_log_dt, ssm_b_re, ssm_b_im,
           ssm_c_re, ssm_c_im, ssm_d, w_branch_a, w_branch_b, w_out, norm_ffn_g,
           w_gate_ffn, w_up_ffn, w_down_ffn, norm_final_g):
    depth = w_in.shape[0]
    assert depth == 1
    nb, seq, d = x_prompt.shape
    ns = x_sample.shape[0]
    row = lambda a: a.reshape(1, -1)

    xp = x_prompt
    xs = x_sample
    gfin = row(norm_final_g)
    p_re, p_im, s_re, s_im, s_v = [], [], [], [], []
    for l in range(depth):
        are, aim, bmat, cmat, wsp, bsp, spw = _prep(
            ssm_lam_re[l], ssm_lam_im[l], ssm_log_dt[l], ssm_b_re[l], ssm_b_im[l],
            ssm_c_re[l], ssm_c_im[l], w_spatial[l], b_spatial[l])
        dsk = row(ssm_d[l])
        shared = dict(gmix=row(norm_mix_g[l]), w_in=w_in[l].astype(BF16), lng=row(ln_v_g[l]),
                      lnb=row(ln_v_b[l]))
        tail = dict(are=are, aim=aim, bmat=bmat, cmat=cmat, dsk=dsk,
                    w_a=w_branch_a[l].astype(BF16), w_b=w_branch_b[l].astype(BF16),
                    w_out=w_out[l].astype(BF16))
        ffn_w = (row(norm_ffn_g[l]), w_gate_ffn[l], w_up_ffn[l], w_down_ffn[l])

        xp, hre, him = _mixer_prompt(xp, **shared, wsp=wsp, bsp=bsp, **tail)
        p_re.append(hre.reshape(nb, SSM_GROUPS, SSM_STATE))
        p_im.append(him.reshape(nb, SSM_GROUPS, SSM_STATE))

        xs, hre, him, v_rows = _mixer_sample(
            xs, state_ssm_re[l].reshape(ns, STATE_W), state_ssm_im[l].reshape(ns, STATE_W),
            **shared, spw=spw, bsp=bsp, **tail)
        s_re.append(hre.reshape(ns, SSM_GROUPS, SSM_STATE))
        s_im.append(him.reshape(ns, SSM_GROUPS, SSM_STATE))
        s_v.append(v_rows)

        xp, xs = _ffn(xp.reshape(nb * seq, d), xs, *ffn_w, gfin, FFN_BLOCK_ROWS)
        xp = xp.reshape(nb, seq, d)

    return (xp, xs, jnp.stack(p_re), jnp.stack(p_im), jnp.stack(s_re),
            jnp.stack(s_im), jnp.stack(s_v))
```

```python
import jax
import jax.numpy as jnp
from jax import lax
from jax.experimental import pallas as pl
from jax.experimental.pallas import tpu as pltpu

F32 = jnp.float32
BF16 = jnp.bfloat16

D_MODEL = 1024
CHUNK = 128
GMLP_GROUPS = 8
GROUP_CH = D_MODEL // GMLP_GROUPS
SSM_WIDTH = 512
SSM_GROUPS = 32
SSM_STATE = 64
SSM_GROUP_CH = 16
LANES = 128
SUBLANES = 8
SLABS = SSM_WIDTH // LANES
SLAB_GROUPS = LANES // SSM_GROUP_CH
SLAB_STATE = SLAB_GROUPS * SSM_STATE
STATE_W = SSM_GROUPS * SSM_STATE
EPS = 1e-6

OFF_U, OFF_V, OFF_S = 0, D_MODEL, 2 * D_MODEL
OFF_GA = 2 * D_MODEL + SSM_WIDTH
OFF_GB = OFF_GA + D_MODEL

ROW_BLOCK = 256
SCAN_STEPS = 16
FFN_BLOCK_ROWS = 512
FFN_SUB_ROWS = 256
VMEM_LIMIT = 56 * 1024 * 1024


def _dot(a, b):
    return jnp.dot(a, b, preferred_element_type=F32)


def _rmsnorm(x, g):
    return x * lax.rsqrt(jnp.mean(x * x, axis=-1, keepdims=True) + EPS) * g


def _layernorm(x, g, b):
    mu = jnp.mean(x, axis=-1, keepdims=True)
    xc = x - mu
    var = jnp.mean(xc * xc, axis=-1, keepdims=True)
    return xc * lax.rsqrt(var + EPS) * g + b


def _const_spec(shape):
    zeros = (0,) * len(shape)
    return pl.BlockSpec(shape, lambda *_: zeros, pipeline_mode=pl.Buffered(1))


STAGE_SLOTS = 8
STAGE_ROWS, STAGE_COLS = 256, 1024


def _cast_pieces(src_hbm, dst_vmem):
    n_rows, n_cols = src_hbm.shape
    assert n_rows % STAGE_ROWS == 0 and n_cols % LANES == 0
    pieces = []
    for r in range(0, n_rows, STAGE_ROWS):
        for c in range(0, n_cols, STAGE_COLS):
            w = min(STAGE_COLS, n_cols - c)
            pieces.append((src_hbm.at[r:r + STAGE_ROWS, c:c + w],
                           dst_vmem.at[r:r + STAGE_ROWS, c:c + w], w))
    return pieces


def _load_cast(pieces, stage_view, sems):
    def copy(k):
        src, _, width = pieces[k]
        return pltpu.make_async_copy(src, stage_view(k % STAGE_SLOTS, width), sems.at[k % STAGE_SLOTS])

    for k in range(min(STAGE_SLOTS, len(pieces))):
        copy(k).start()
    for k, (_, dst, width) in enumerate(pieces):
        copy(k).wait()
        dst[...] = stage_view(k % STAGE_SLOTS, width)[...].astype(BF16)
        if k + STAGE_SLOTS < len(pieces):
            copy(k + STAGE_SLOTS).start()


def _dot_exact(a, b):
    return jnp.dot(a, b, preferred_element_type=F32, precision=lax.Precision.HIGHEST)


def _group_mask(shape, row_div, lane_div):
    rows = lax.broadcasted_iota(jnp.int32, shape, 0) // row_div
    lanes = lax.broadcasted_iota(jnp.int32, shape, 1) // lane_div
    return rows == lanes


def _prep_kernel(lre, lim, ldt, bre, bim, cre_in, cim_in, wsp_in, bsp_in,
                 are_o, aim_o, bmat_o, cmat_o, wsp_o, bsp_o, spw_o):
    def slab_rows(row_of_group):
        return jnp.concatenate(
            [jnp.concatenate([row_of_group(j * SLAB_GROUPS + k) for k in range(SLAB_GROUPS)], axis=1)
             for j in range(SLABS)], axis=0)

    lr = slab_rows(lambda g: lre[g:g + 1, :])
    li = slab_rows(lambda g: lim[g:g + 1, :])
    dt = jnp.exp(slab_rows(lambda g: jnp.broadcast_to(ldt[0:1, g:g + 1], (1, SSM_STATE))))
    mag = jnp.exp(lr * dt)
    a_re, a_im = mag * jnp.cos(li * dt), mag * jnp.sin(li * dt)
    n_re, n_im = a_re - 1.0, a_im
    den = lr * lr + li * li
    k_re = (n_re * lr + n_im * li) / den
    k_im = (n_im * lr - n_re * li) / den

    rep_h = (lax.broadcasted_iota(jnp.int32, (SSM_GROUP_CH, LANES), 1) % SSM_GROUP_CH
             == lax.broadcasted_iota(jnp.int32, (SSM_GROUP_CH, LANES), 0)).astype(F32)
    rep_p = (lax.broadcasted_iota(jnp.int32, (SSM_STATE, SLAB_STATE), 1) % SSM_STATE
             == lax.broadcasted_iota(jnp.int32, (SSM_STATE, SLAB_STATE), 0)).astype(F32)
    tb_re = _dot_exact(bre[...], rep_h)
    tb_im = _dot_exact(bim[...], rep_h)
    tc_re = _dot_exact(cre_in[...], rep_p)
    tc_im = _dot_exact(cim_in[...], rep_p)
    mask_b = _group_mask((SLAB_STATE, LANES), SSM_STATE, SSM_GROUP_CH)
    mask_c = _group_mask((LANES, SLAB_STATE), SSM_GROUP_CH, SSM_STATE)

    for j in range(SLABS):
        lanes = slice(j * SLAB_STATE, (j + 1) * SLAB_STATE)
        are_o[:, lanes] = jnp.broadcast_to(a_re[j:j + 1, :], (SUBLANES, SLAB_STATE))
        aim_o[:, lanes] = jnp.broadcast_to(a_im[j:j + 1, :], (SUBLANES, SLAB_STATE))
        rows = slice(j * SLAB_STATE, (j + 1) * SLAB_STATE)
        m_re = jnp.where(mask_b, tb_re[rows], 0.0).T
        m_im = jnp.where(mask_b, tb_im[rows], 0.0).T
        c_re, c_im = k_re[j:j + 1, :], k_im[j:j + 1, :]
        bmat_o[j] = jnp.concatenate([c_re * m_re - c_im * m_im, c_re * m_im + c_im * m_re],
                                    axis=1).astype(BF16)
        rows = slice(j * LANES, (j + 1) * LANES)
        read = jnp.concatenate([jnp.where(mask_c, tc_re[rows], 0.0),
                                jnp.where(mask_c, -tc_im[rows], 0.0)], axis=1)
        cmat_o[j] = read.T.astype(BF16)

    tril = (lax.broadcasted_iota(jnp.int32, (CHUNK, CHUNK), 0)
            >= lax.broadcasted_iota(jnp.int32, (CHUNK, CHUNK), 1))
    spw = []
    for g in range(GMLP_GROUPS):
        w = wsp_in[g]
        wsp_o[g] = jnp.where(tril, w, 0.0).astype(BF16)
        spw.append(jnp.broadcast_to(w[0:1, 0:1], (1, GROUP_CH)))
    spw_o[...] = jnp.concatenate(spw, axis=1)
    sel = _group_mask((D_MODEL, GMLP_GROUPS), GROUP_CH, 1).astype(F32)
    bsp_o[...] = _dot_exact(sel, bsp_in[...]).T


def _prep(lam_re, lam_im, log_dt, b_re, b_im, c_re, c_im, w_spatial, b_spatial):
    g, p, h = b_re.shape
    return pl.pallas_call(
        _prep_kernel,
        out_shape=(jax.ShapeDtypeStruct((SUBLANES, STATE_W), F32),
                   jax.ShapeDtypeStruct((SUBLANES, STATE_W), F32),
                   jax.ShapeDtypeStruct((SLABS, LANES, 2 * SLAB_STATE), BF16),
                   jax.ShapeDtypeStruct((SLABS, 2 * SLAB_STATE, LANES), BF16),
                   jax.ShapeDtypeStruct(w_spatial.shape, BF16),
                   jax.ShapeDtypeStruct((CHUNK, D_MODEL), F32),
                   jax.ShapeDtypeStruct((1, D_MODEL), F32)),
        name="prep",
    )(lam_re, lam_im, log_dt.reshape(1, g), b_re.reshape(g * p, h), b_im.reshape(g * p, h),
      c_re.reshape(g * h, p), c_im.reshape(g * h, p), w_spatial, b_spatial)


def _mixer_prompt_kernel(x_ref, gmix, w_in, lng, lnb, wsp, bsp, are, aim, bmat, cmat, dsk,
                         w_a, w_b, w_out,
                         x1_ref, hre_o, him_o,
                         s_tm, ys_tm, ya_s, hn_s, bu, hbuf, st_re, st_im):
    i = pl.program_id(0)
    nb = x_ref.shape[0]
    rows = nb * CHUNK
    per_block = ROW_BLOCK // CHUNK

    @pl.when(i == 0)
    def _():
        st_re[...] = jnp.zeros_like(st_re)
        st_im[...] = jnp.zeros_like(st_im)

    def project(blk):
        r = slice(blk * ROW_BLOCK, (blk + 1) * ROW_BLOCK)
        x = jnp.concatenate([x_ref[blk * per_block + k] for k in range(per_block)], axis=0)
        hn = _rmsnorm(x, gmix[...]).astype(BF16)
        hn_s[r, :] = hn
        zu = _dot(hn, w_in[:, OFF_U:OFF_U + D_MODEL])
        zv = _dot(hn, w_in[:, OFF_V:OFF_V + D_MODEL])
        s = _dot(hn, w_in[:, OFF_S:OFF_S + SSM_WIDTH])
        for k in range(per_block):
            b = blk * per_block + k
            for j in range(SLABS):
                s_tm[j, pl.ds(b, CHUNK, stride=nb), :] = s[k * CHUNK:(k + 1) * CHUNK,
                                                          j * LANES:(j + 1) * LANES]
        return zu, zv

    def gate(blk, zu, zv):
        u = jax.nn.gelu(zu)
        vb = _layernorm(jax.nn.gelu(zv), lng[...], lnb[...]).astype(BF16)
        mixed = []
        for k in range(per_block):
            cols = [_dot(wsp[g], vb[k * CHUNK:(k + 1) * CHUNK, g * GROUP_CH:(g + 1) * GROUP_CH])
                    for g in range(GMLP_GROUPS)]
            mixed.append(jnp.concatenate(cols, axis=1) + bsp[...])
        gated = (u * jnp.concatenate(mixed, axis=0)).astype(BF16)
        ya_s[blk * ROW_BLOCK:(blk + 1) * ROW_BLOCK, :] = _dot(gated, w_a[...])

    n_blk = rows // ROW_BLOCK
    z = project(0)
    for blk in range(n_blk):
        z_next = project(blk + 1) if blk + 1 < n_blk else None
        gate(blk, *z)
        z = z_next

    sub_rows = SCAN_STEPS * nb

    def project_in(q):
        for j in range(SLABS):
            s_blk = s_tm[j, q * sub_rows:(q + 1) * sub_rows, :]
            bu[q % 2, j] = _dot(s_blk.astype(BF16), bmat[j])

    def recur(q):
        for j in range(SLABS):
            lanes = slice(j * SLAB_STATE, (j + 1) * SLAB_STATE)
            a_re, a_im = are[:, lanes], aim[:, lanes]
            h_re, h_im = st_re[:, lanes], st_im[:, lanes]
            for t in range(0, SCAN_STEPS, 2):
                pair = []
                for r in (slice(t * nb, (t + 1) * nb), slice((t + 1) * nb, (t + 2) * nb)):
                    n_re = a_re * h_re - a_im * h_im + bu[q % 2, j, r, 0:SLAB_STATE]
                    n_im = a_re * h_im + a_im * h_re + bu[q % 2, j, r, SLAB_STATE:2 * SLAB_STATE]
                    h_re, h_im = n_re, n_im
                    pair.append(jnp.concatenate([n_re, n_im], axis=1))
                hbuf[q % 2, j, t * nb:(t + 2) * nb, :] = jnp.concatenate(pair, axis=0).astype(BF16)
            st_re[:, lanes] = h_re
            st_im[:, lanes] = h_im

    def project_out(q):
        for j in range(SLABS):
            r = slice(q * sub_rows, (q + 1) * sub_rows)
            d_j = dsk[:, j * LANES:(j + 1) * LANES]
            ys_tm[j, r, :] = _dot(hbuf[q % 2, j], cmat[j]) + d_j * s_tm[j, r, :]

    n_sub = rows // sub_rows
    project_in(0)
    for q in range(n_sub):
        if q + 1 < n_sub:
            project_in(q + 1)
        recur(q)
        if q >= 1:
            project_out(q - 1)
    project_out(n_sub - 1)

    def phase3(blk, carry):
        r0 = pl.multiple_of(blk * ROW_BLOCK, ROW_BLOCK)
        ys = jnp.concatenate(
            [jnp.concatenate([ys_tm[j, pl.ds(blk * per_block + k, CHUNK, stride=nb), :]
                              for j in range(SLABS)], axis=1)
             for k in range(per_block)], axis=0)
        pb = _dot(ys.astype(BF16), w_b[...])
        yb = pb[:, :D_MODEL] * jax.nn.sigmoid(pb[:, D_MODEL:])
        hn = hn_s[pl.ds(r0, ROW_BLOCK), :]
        ga = jax.nn.sigmoid(_dot(hn, w_in[:, OFF_GA:OFF_GA + D_MODEL]))
        gb = jax.nn.sigmoid(_dot(hn, w_in[:, OFF_GB:OFF_GB + D_MODEL]))
        merged = (ga * ya_s[pl.ds(r0, ROW_BLOCK), :] + gb * yb).astype(BF16)
        y = _dot(merged, w_out[...])
        for k in range(per_block):
            b = blk * per_block + k
            x1_ref[b] = x_ref[b] + y[k * CHUNK:(k + 1) * CHUNK]
        return carry

    lax.fori_loop(0, rows // ROW_BLOCK, phase3, 0, unroll=2)

    @pl.when(i == pl.num_programs(0) - 1)
    def _():
        hre_o[...] = st_re[...]
        him_o[...] = st_im[...]


def _mixer_prompt(x, gmix, w_in, lng, lnb, wsp, bsp, are, aim, bmat, cmat, dsk, w_a, w_b, w_out):
    nb, seq, d = x.shape
    rows = nb * CHUNK
    consts = (gmix, w_in, lng, lnb, wsp, bsp, are, aim, bmat, cmat, dsk, w_a, w_b, w_out)
    x_spec = pl.BlockSpec((nb, CHUNK, d), lambda i: (0, i, 0))
    st_spec = pl.BlockSpec((nb, STATE_W), lambda i: (0, 0))
    return pl.pallas_call(
        _mixer_prompt_kernel,
        grid=(seq // CHUNK,),
        in_specs=[x_spec] + [_const_spec(c.shape) for c in consts],
        out_specs=(x_spec, st_spec, st_spec),
        out_shape=(jax.ShapeDtypeStruct(x.shape, F32),
                   jax.ShapeDtypeStruct((nb, STATE_W), F32),
                   jax.ShapeDtypeStruct((nb, STATE_W), F32)),
        scratch_shapes=[
            pltpu.VMEM((SLABS, rows, LANES), F32),
            pltpu.VMEM((SLABS, rows, LANES), F32),
            pltpu.VMEM((rows, d), F32),
            pltpu.VMEM((rows, d), BF16),
            pltpu.VMEM((2, SLABS, SCAN_STEPS * nb, 2 * SLAB_STATE), F32),
            pltpu.VMEM((2, SLABS, SCAN_STEPS * nb, 2 * SLAB_STATE), BF16),
            pltpu.VMEM((nb, STATE_W), F32),
            pltpu.VMEM((nb, STATE_W), F32),
        ],
        compiler_params=pltpu.CompilerParams(dimension_semantics=("arbitrary",),
                                             vmem_limit_bytes=VMEM_LIMIT),
        name="mixer_prompt",
    )(x, *consts)


def _mixer_sample_kernel(x_ref, h0re, h0im, gmix, w_in, lng, lnb, spw, bsp, are, aim, bmat, cmat,
                         dsk, w_a, w_b, w_out,
                         x1_ref, hre_o, him_o, v_o):
    x = x_ref[:, 0, :]
    hn = _rmsnorm(x, gmix[...]).astype(BF16)
    u = jax.nn.gelu(_dot(hn, w_in[:, OFF_U:OFF_U + D_MODEL]))
    v = _layernorm(jax.nn.gelu(_dot(hn, w_in[:, OFF_V:OFF_V + D_MODEL])), lng[...], lnb[...])
    v_o[:, 0, :] = v
    mixed = v * spw[...] + bsp[0:1, :]
    ya = _dot((u * mixed).astype(BF16), w_a[...])
    s = _dot(hn, w_in[:, OFF_S:OFF_S + SSM_WIDTH])
    ys = []
    for j in range(SLABS):
        lanes = slice(j * SLAB_STATE, (j + 1) * SLAB_STATE)
        s_j = s[:, j * LANES:(j + 1) * LANES]
        bu = _dot(s_j.astype(BF16), bmat[j])
        a_re, a_im = are[0:1, lanes], aim[0:1, lanes]
        p_re, p_im = h0re[:, lanes], h0im[:, lanes]
        n_re = a_re * p_re - a_im * p_im + bu[:, 0:SLAB_STATE]
        n_im = a_re * p_im + a_im * p_re + bu[:, SLAB_STATE:]
        hre_o[:, lanes] = n_re
        him_o[:, lanes] = n_im
        h = jnp.concatenate([n_re, n_im], axis=1).astype(BF16)
        ys.append(_dot(h, cmat[j]) + dsk[:, j * LANES:(j + 1) * LANES] * s_j)
    pb = _dot(jnp.concatenate(ys, axis=1).astype(BF16), w_b[...])
    yb = pb[:, :D_MODEL] * jax.nn.sigmoid(pb[:, D_MODEL:])
    ga = jax.nn.sigmoid(_dot(hn, w_in[:, OFF_GA:OFF_GA + D_MODEL]))
    gb = jax.nn.sigmoid(_dot(hn, w_in[:, OFF_GB:OFF_GB + D_MODEL]))
    merged = (ga * ya + gb * yb).astype(BF16)
    x1_ref[...] = x + _dot(merged, w_out[...])


def _mixer_sample(x, h0re, h0im, gmix, w_in, lng, lnb, spw, bsp, are, aim, bmat, cmat, dsk,
                  w_a, w_b, w_out):
    n, _, d = x.shape
    return pl.pallas_call(
        _mixer_sample_kernel,
        out_shape=(jax.ShapeDtypeStruct((n, d), F32),
                   jax.ShapeDtypeStruct((n, STATE_W), F32),
                   jax.ShapeDtypeStruct((n, STATE_W), F32),
                   jax.ShapeDtypeStruct((n, 1, d), F32)),
        compiler_params=pltpu.CompilerParams(vmem_limit_bytes=VMEM_LIMIT),
        name="mixer_sample",
    )(x, h0re, h0im, gmix, w_in, lng, lnb, spw, bsp, are, aim, bmat, cmat, dsk, w_a, w_b, w_out)


def _ffn_rows(x, gffn, w_gate, w_up, w_down, gfin):
    h = _rmsnorm(x, gffn[...]).astype(BF16)
    act = (jax.nn.silu(_dot(h, w_gate[...])) * _dot(h, w_up[...])).astype(BF16)
    x2 = x + _dot(act, w_down[...])
    return _rmsnorm(x2, gfin[...])


def _ffn_kernel(x_ref, xs_ref, gffn, wg_hbm, wu_hbm, wd_hbm, gfin, y_ref, ys_ref,
                w_gate, w_up, w_down, stage, sems):
    @pl.when(pl.program_id(0) == 0)
    def _():
        pieces = (_cast_pieces(wg_hbm, w_gate) + _cast_pieces(wu_hbm, w_up)
                  + _cast_pieces(wd_hbm, w_down))
        _load_cast(pieces, lambda slot, width: stage.at[slot, :, 0:width], sems)

    params = (gffn, w_gate, w_up, w_down, gfin)
    for r0 in range(0, x_ref.shape[0], FFN_SUB_ROWS):
        r = slice(r0, r0 + FFN_SUB_ROWS)
        y_ref[r, :] = _ffn_rows(x_ref[r, :], *params)

    @pl.when(pl.program_id(0) == pl.num_programs(0) - 1)
    def _():
        ys_ref[:, 0, :] = _ffn_rows(xs_ref[...], *params)


def _ffn(x, xs, gffn, w_gate, w_up, w_down, gfin, block_rows):
    n, d = x.shape
    ns = xs.shape[0]
    row_spec = pl.BlockSpec((block_rows, d), lambda i: (i, 0))
    hbm_spec = pl.BlockSpec(memory_space=pl.ANY)
    return pl.pallas_call(
        _ffn_kernel,
        grid=(n // block_rows,),
        in_specs=[row_spec, _const_spec(xs.shape), _const_spec(gffn.shape), hbm_spec, hbm_spec,
                  hbm_spec, _const_spec(gfin.shape)],
        out_specs=(row_spec, pl.BlockSpec((ns, 1, d), lambda i: (0, 0, 0))),
        out_shape=(jax.ShapeDtypeStruct((n, d), F32), jax.ShapeDtypeStruct((ns, 1, d), F32)),
        scratch_shapes=[pltpu.VMEM(w_gate.shape, BF16), pltpu.VMEM(w_up.shape, BF16),
                        pltpu.VMEM(w_down.shape, BF16),
                        pltpu.VMEM((STAGE_SLOTS, STAGE_ROWS, STAGE_COLS), F32),
                        pltpu.SemaphoreType.DMA((STAGE_SLOTS,))],
        compiler_params=pltpu.CompilerParams(dimension_semantics=("arbitrary",),
                                             vmem_limit_bytes=VMEM_LIMIT),
        name="ffn",
    )(x, xs, gffn, w_gate, w_up, w_down, gfin)


def kernel(x_prompt, x_sample, state_ssm_re, state_ssm_im, norm_mix_g, w_in, ln_v_g, ln_v_b,
           w_spatial, b_spatial, ssm_lam_re, ssm_lam_im, ssm_log_dt, ssm_b_re, ssm_b_im,
           ssm_c_re, ssm_c_im, ssm_d, w_branch_a, w_branch_b, w_out, norm_ffn_g,
           w_gate_ffn, w_up_ffn, w_down_ffn, norm_final_g):
    depth = w_in.shape[0]
    assert depth == 1
    nb, seq, d = x_prompt.shape
    ns = x_sample.shape[0]
    row = lambda a: a.reshape(1, -1)

    xp = x_prompt
    xs = x_sample
    gfin = row(norm_final_g)
    p_re, p_im, s_re, s_im, s_v = [], [], [], [], []
    for l in range(depth):
        are, aim, bmat, cmat, wsp, bsp, spw = _prep(
            ssm_lam_re[l], ssm_lam_im[l], ssm_log_dt[l], ssm_b_re[l], ssm_b_im[l],
            ssm_c_re[l], ssm_c_im[l], w_spatial[l], b_spatial[l])
        dsk = row(ssm_d[l])
        shared = dict(gmix=row(norm_mix_g[l]), w_in=w_in[l].astype(BF16), lng=row(ln_v_g[l]),
                      lnb=row(ln_v_b[l]))
        tail = dict(are=are, aim=aim, bmat=bmat, cmat=cmat, dsk=dsk,
                    w_a=w_branch_a[l].astype(BF16), w_b=w_branch_b[l].astype(BF16),
                    w_out=w_out[l].astype(BF16))
        ffn_w = (row(norm_ffn_g[l]), w_gate_ffn[l], w_up_ffn[l], w_down_ffn[l])

        xp, hre, him = _mixer_prompt(xp, **shared, wsp=wsp, bsp=bsp, **tail)
        p_re.append(hre.reshape(nb, SSM_GROUPS, SSM_STATE))
        p_im.append(him.reshape(nb, SSM_GROUPS, SSM_STATE))

        xs, hre, him, v_rows = _mixer_sample(
            xs, state_ssm_re[l].reshape(ns, STATE_W), state_ssm_im[l].reshape(ns, STATE_W),
            **shared, spw=spw, bsp=bsp, **tail)
        s_re.append(hre.reshape(ns, SSM_GROUPS, SSM_STATE))
        s_im.append(him.reshape(ns, SSM_GROUPS, SSM_STATE))
        s_v.append(v_rows)

        xp, xs = _ffn(xp.reshape(nb * seq, d), xs, *ffn_w, gfin, FFN_BLOCK_ROWS)
        xp = xp.reshape(nb, seq, d)

    return (xp, xs, jnp.stack(p_re), jnp.stack(p_im), jnp.stack(s_re),
            jnp.stack(s_im), jnp.stack(s_v))
```

```python
import jax
import jax.numpy as jnp
from jax import lax
from jax.experimental import pallas as pl
from jax.experimental.pallas import tpu as pltpu

F32 = jnp.float32
BF16 = jnp.bfloat16

D_MODEL = 1024
CHUNK = 128
GMLP_GROUPS = 8
GROUP_CH = D_MODEL // GMLP_GROUPS
SSM_WIDTH = 512
SSM_GROUPS = 32
SSM_STATE = 64
SSM_GROUP_CH = 16
LANES = 128
SUBLANES = 8
SLABS = SSM_WIDTH // LANES
SLAB_GROUPS = LANES // SSM_GROUP_CH
SLAB_STATE = SLAB_GROUPS * SSM_STATE
STATE_W = SSM_GROUPS * SSM_STATE
EPS = 1e-6

OFF_U, OFF_V, OFF_S = 0, D_MODEL, 2 * D_MODEL
OFF_GA = 2 * D_MODEL + SSM_WIDTH
OFF_GB = OFF_GA + D_MODEL

ROW_BLOCK = 256
SCAN_STEPS = 16
FFN_BLOCK_ROWS = 512
FFN_SUB_ROWS = 256
VMEM_LIMIT = 56 * 1024 * 1024


def _dot(a, b):
    return jnp.dot(a, b, preferred_element_type=F32)


def _rmsnorm(x, g):
    return x * lax.rsqrt(jnp.mean(x * x, axis=-1, keepdims=True) + EPS) * g


def _layernorm(x, g, b):
    mu = jnp.mean(x, axis=-1, keepdims=True)
    xc = x - mu
    var = jnp.mean(xc * xc, axis=-1, keepdims=True)
    return xc * lax.rsqrt(var + EPS) * g + b


def _const_spec(shape):
    zeros = (0,) * len(shape)
    return pl.BlockSpec(shape, lambda *_: zeros, pipeline_mode=pl.Buffered(1))


STAGE_SLOTS = 8
STAGE_ROWS, STAGE_COLS = 256, 1024


def _cast_pieces(src_hbm, dst_vmem):
    n_rows, n_cols = src_hbm.shape
    assert n_rows % STAGE_ROWS == 0 and n_cols % LANES == 0
    pieces = []
    for r in range(0, n_rows, STAGE_ROWS):
        for c in range(0, n_cols, STAGE_COLS):
            w = min(STAGE_COLS, n_cols - c)
            pieces.append((src_hbm.at[r:r + STAGE_ROWS, c:c + w],
                           dst_vmem.at[r:r + STAGE_ROWS, c:c + w], w))
    return pieces


def _load_cast(pieces, stage_view, sems):
    def copy(k):
        src, _, width = pieces[k]
        slot = k % STAGE_SLOTS
        return pltpu.make_async_copy(src, stage_view(slot, width), sems.at[slot])

    for k in range(min(STAGE_SLOTS, len(pieces))):
        copy(k).start()
    for k, (_, dst, width) in enumerate(pieces):
        copy(k).wait()
        dst[...] = stage_view(k % STAGE_SLOTS, width)[...].astype(BF16)
        if k + STAGE_SLOTS < len(pieces):
            copy(k + STAGE_SLOTS).start()


def _dot_exact(a, b):
    return jnp.dot(a, b, preferred_element_type=F32, precision=lax.Precision.HIGHEST)


def _group_mask(shape, row_div, lane_div):
    rows = lax.broadcasted_iota(jnp.int32, shape, 0) // row_div
    lanes = lax.broadcasted_iota(jnp.int32, shape, 1) // lane_div
    return rows == lanes


def _prep_kernel(lre, lim, ldt, bre, bim, cre_in, cim_in, wsp_in, bsp_in,
                 are_o, aim_o, bmat_o, cmat_o, wsp_o, bsp_o, spw_o):
    def slab_rows(row_of_group):
        slabs = [[row_of_group(j * SLAB_GROUPS + k) for k in range(SLAB_GROUPS)]
                 for j in range(SLABS)]
        return jnp.concatenate([jnp.concatenate(rows, axis=1) for rows in slabs], axis=0)

    lr = slab_rows(lambda g: lre[g:g + 1, :])
    li = slab_rows(lambda g: lim[g:g + 1, :])
    dt = jnp.exp(slab_rows(lambda g: jnp.broadcast_to(ldt[0:1, g:g + 1], (1, SSM_STATE))))
    mag = jnp.exp(lr * dt)
    a_re, a_im = mag * jnp.cos(li * dt), mag * jnp.sin(li * dt)
    n_re, n_im = a_re - 1.0, a_im
    den = lr * lr + li * li
    k_re = (n_re * lr + n_im * li) / den
    k_im = (n_im * lr - n_re * li) / den

    rep_h = (lax.broadcasted_iota(jnp.int32, (SSM_GROUP_CH, LANES), 1) % SSM_GROUP_CH
             == lax.broadcasted_iota(jnp.int32, (SSM_GROUP_CH, LANES), 0)).astype(F32)
    rep_p = (lax.broadcasted_iota(jnp.int32, (SSM_STATE, SLAB_STATE), 1) % SSM_STATE
             == lax.broadcasted_iota(jnp.int32, (SSM_STATE, SLAB_STATE), 0)).astype(F32)
    tb_re = _dot_exact(bre[...], rep_h)
    tb_im = _dot_exact(bim[...], rep_h)
    tc_re = _dot_exact(cre_in[...], rep_p)
    tc_im = _dot_exact(cim_in[...], rep_p)
    mask_b = _group_mask((SLAB_STATE, LANES), SSM_STATE, SSM_GROUP_CH)
    mask_c = _group_mask((LANES, SLAB_STATE), SSM_GROUP_CH, SSM_STATE)

    for j in range(SLABS):
        lanes = slice(j * SLAB_STATE, (j + 1) * SLAB_STATE)
        are_o[:, lanes] = jnp.broadcast_to(a_re[j:j + 1, :], (SUBLANES, SLAB_STATE))
        aim_o[:, lanes] = jnp.broadcast_to(a_im[j:j + 1, :], (SUBLANES, SLAB_STATE))
        rows = slice(j * SLAB_STATE, (j + 1) * SLAB_STATE)
        m_re = jnp.where(mask_b, tb_re[rows], 0.0).T
        m_im = jnp.where(mask_b, tb_im[rows], 0.0).T
        c_re, c_im = k_re[j:j + 1, :], k_im[j:j + 1, :]
        bmat_o[j] = jnp.concatenate([c_re * m_re - c_im * m_im, c_re * m_im + c_im * m_re],
                                    axis=1).astype(BF16)
        rows = slice(j * LANES, (j + 1) * LANES)
        read = jnp.concatenate([jnp.where(mask_c, tc_re[rows], 0.0),
                                jnp.where(mask_c, -tc_im[rows], 0.0)], axis=1)
        cmat_o[j] = read.T.astype(BF16)

    tril = (lax.broadcasted_iota(jnp.int32, (CHUNK, CHUNK), 0)
            >= lax.broadcasted_iota(jnp.int32, (CHUNK, CHUNK), 1))
    spw = []
    for g in range(GMLP_GROUPS):
        w = wsp_in[g]
        wsp_o[g] = jnp.where(tril, w, 0.0).astype(BF16)
        spw.append(jnp.broadcast_to(w[0:1, 0:1], (1, GROUP_CH)))
    spw_o[...] = jnp.concatenate(spw, axis=1)
    sel = _group_mask((D_MODEL, GMLP_GROUPS), GROUP_CH, 1).astype(F32)
    bsp_o[...] = _dot_exact(sel, bsp_in[...]).T


def _prep(lam_re, lam_im, log_dt, b_re, b_im, c_re, c_im, w_spatial, b_spatial):
    g, p, h = b_re.shape
    return pl.pallas_call(
        _prep_kernel,
        out_shape=(jax.ShapeDtypeStruct((SUBLANES, STATE_W), F32),
                   jax.ShapeDtypeStruct((SUBLANES, STATE_W), F32),
                   jax.ShapeDtypeStruct((SLABS, LANES, 2 * SLAB_STATE), BF16),
                   jax.ShapeDtypeStruct((SLABS, 2 * SLAB_STATE, LANES), BF16),
                   jax.ShapeDtypeStruct(w_spatial.shape, BF16),
                   jax.ShapeDtypeStruct((CHUNK, D_MODEL), F32),
                   jax.ShapeDtypeStruct((1, D_MODEL), F32)),
        name="prep",
    )(lam_re, lam_im, log_dt.reshape(1, g), b_re.reshape(g * p, h), b_im.reshape(g * p, h),
      c_re.reshape(g * h, p), c_im.reshape(g * h, p), w_spatial, b_spatial)


def _mixer_prompt_kernel(x_ref, gmix, w_in, lng, lnb, wsp, bsp, are, aim, bmat, cmat, dsk,
                         w_a, w_b, w_out,
                         x1_ref, hre_o, him_o,
                         s_tm, ys_tm, ya_s, hn_s, bu, hbuf, st_re, st_im):
    i = pl.program_id(0)
    nb = x_ref.shape[0]
    rows = nb * CHUNK
    per_block = ROW_BLOCK // CHUNK

    @pl.when(i == 0)
    def _():
        st_re[...] = jnp.zeros_like(st_re)
        st_im[...] = jnp.zeros_like(st_im)

    def project(blk):
        r = slice(blk * ROW_BLOCK, (blk + 1) * ROW_BLOCK)
        x = jnp.concatenate([x_ref[blk * per_block + k] for k in range(per_block)], axis=0)
        hn = _rmsnorm(x, gmix[...]).astype(BF16)
        hn_s[r, :] = hn
        zu = _dot(hn, w_in[:, OFF_U:OFF_U + D_MODEL])
        zv = _dot(hn, w_in[:, OFF_V:OFF_V + D_MODEL])
        s = _dot(hn, w_in[:, OFF_S:OFF_S + SSM_WIDTH])
        for k in range(per_block):
            b = blk * per_block + k
            for j in range(SLABS):
                s_tm[j, pl.ds(b, CHUNK, stride=nb), :] = s[k * CHUNK:(k + 1) * CHUNK,
                                                          j * LANES:(j + 1) * LANES]
        return zu, zv

    def gate(blk, zu, zv):
        u = jax.nn.gelu(zu)
        vb = _layernorm(jax.nn.gelu(zv), lng[...], lnb[...]).astype(BF16)
        wide = [_dot(wsp[g], jnp.concatenate(
                    [vb[k * CHUNK:(k + 1) * CHUNK, g * GROUP_CH:(g + 1) * GROUP_CH]
                     for k in range(per_block)], axis=1))
                for g in range(GMLP_GROUPS)]
        mixed = [jnp.concatenate([w[:, k * GROUP_CH:(k + 1) * GROUP_CH] for w in wide], axis=1)
                 + bsp[...] for k in range(per_block)]
        gated = (u * jnp.concatenate(mixed, axis=0)).astype(BF16)
        ya_s[blk * ROW_BLOCK:(blk + 1) * ROW_BLOCK, :] = _dot(gated, w_a[...])

    n_blk = rows // ROW_BLOCK
    z = project(0)
    for blk in range(n_blk):
        z_next = project(blk + 1) if blk + 1 < n_blk else None
        gate(blk, *z)
        z = z_next

    sub_rows = SCAN_STEPS * nb

    def project_in(q):
        for j in range(SLABS):
            s_blk = s_tm[j, q * sub_rows:(q + 1) * sub_rows, :]
            bu[q % 2, j] = _dot(s_blk.astype(BF16), bmat[j])

    def recur(q):
        for j in range(SLABS):
            lanes = slice(j * SLAB_STATE, (j + 1) * SLAB_STATE)
            a_re, a_im = are[:, lanes], aim[:, lanes]
            h_re, h_im = st_re[:, lanes], st_im[:, lanes]
            for t in range(0, SCAN_STEPS, 2):
                pair = []
                for r in (slice(t * nb, (t + 1) * nb), slice((t + 1) * nb, (t + 2) * nb)):
                    n_re = a_re * h_re - a_im * h_im + bu[q % 2, j, r, 0:SLAB_STATE]
                    n_im = a_re * h_im + a_im * h_re + bu[q % 2, j, r, SLAB_STATE:2 * SLAB_STATE]
                    h_re, h_im = n_re, n_im
                    pair.append(jnp.concatenate([n_re, n_im], axis=1))
                hbuf[q % 2, j, t * nb:(t + 2) * nb, :] = jnp.concatenate(pair, axis=0).astype(BF16)
            st_re[:, lanes] = h_re
            st_im[:, lanes] = h_im

    def project_out(q):
        for j in range(SLABS):
            r = slice(q * sub_rows, (q + 1) * sub_rows)
            d_j = dsk[:, j * LANES:(j + 1) * LANES]
            ys_tm[j, r, :] = _dot(hbuf[q % 2, j], cmat[j]) + d_j * s_tm[j, r, :]

    n_sub = rows // sub_rows
    project_in(0)
    for q in range(n_sub):
        if q + 1 < n_sub:
            project_in(q + 1)
        recur(q)
        if q >= 1:
            project_out(q - 1)
    project_out(n_sub - 1)

    def merge(blk):
        r = slice(blk * ROW_BLOCK, (blk + 1) * ROW_BLOCK)
        ys = jnp.concatenate(
            [jnp.concatenate([ys_tm[j, pl.ds(blk * per_block + k, CHUNK, stride=nb), :]
                              for j in range(SLABS)], axis=1)
             for k in range(per_block)], axis=0)
        pb = _dot(ys.astype(BF16), w_b[...])
        yb = pb[:, :D_MODEL] * jax.nn.sigmoid(pb[:, D_MODEL:])
        hn = hn_s[r, :]
        ga = jax.nn.sigmoid(_dot(hn, w_in[:, OFF_GA:OFF_GA + D_MODEL]))
        gb = jax.nn.sigmoid(_dot(hn, w_in[:, OFF_GB:OFF_GB + D_MODEL]))
        merged = (ga * ya_s[r, :] + gb * yb).astype(BF16)
        y = _dot(merged, w_out[...])
        for k in range(per_block):
            b = blk * per_block + k
            x1_ref[b] = x_ref[b] + y[k * CHUNK:(k + 1) * CHUNK]

    for blk in range(n_blk):
        merge(blk)

    @pl.when(i == pl.num_programs(0) - 1)
    def _():
        hre_o[...] = st_re[...]
        him_o[...] = st_im[...]


def _mixer_prompt(x, gmix, w_in, lng, lnb, wsp, bsp, are, aim, bmat, cmat, dsk, w_a, w_b, w_out):
    nb, seq, d = x.shape
    rows = nb * CHUNK
    consts = (gmix, w_in, lng, lnb, wsp, bsp, are, aim, bmat, cmat, dsk, w_a, w_b, w_out)
    x_spec = pl.BlockSpec((nb, CHUNK, d), lambda i: (0, i, 0))
    st_spec = pl.BlockSpec((nb, STATE_W), lambda i: (0, 0))
    return pl.pallas_call(
        _mixer_prompt_kernel,
        grid=(seq // CHUNK,),
        in_specs=[x_spec] + [_const_spec(c.shape) for c in consts],
        out_specs=(x_spec, st_spec, st_spec),
        out_shape=(jax.ShapeDtypeStruct(x.shape, F32),
                   jax.ShapeDtypeStruct((nb, STATE_W), F32),
                   jax.ShapeDtypeStruct((nb, STATE_W), F32)),
        scratch_shapes=[
            pltpu.VMEM((SLABS, rows, LANES), F32),
            pltpu.VMEM((SLABS, rows, LANES), F32),
            pltpu.VMEM((rows, d), F32),
            pltpu.VMEM((rows, d), BF16),
            pltpu.VMEM((2, SLABS, SCAN_STEPS * nb, 2 * SLAB_STATE), F32),
            pltpu.VMEM((2, SLABS, SCAN_STEPS * nb, 2 * SLAB_STATE), BF16),
            pltpu.VMEM((nb, STATE_W), F32),
            pltpu.VMEM((nb, STATE_W), F32),
        ],
        compiler_params=pltpu.CompilerParams(dimension_semantics=("arbitrary",),
                                             vmem_limit_bytes=VMEM_LIMIT),
        name="mixer_prompt",
    )(x, *consts)


def _mixer_sample_kernel(x_ref, h0re, h0im, gmix, w_in, lng, lnb, spw, bsp, are, aim, bmat, cmat,
                         dsk, w_a, w_b, w_out,
                         x1_ref, hre_o, him_o, v_o):
    x = x_ref[:, 0, :]
    hn = _rmsnorm(x, gmix[...]).astype(BF16)
    u = jax.nn.gelu(_dot(hn, w_in[:, OFF_U:OFF_U + D_MODEL]))
    v = _layernorm(jax.nn.gelu(_dot(hn, w_in[:, OFF_V:OFF_V + D_MODEL])), lng[...], lnb[...])
    v_o[:, 0, :] = v
    mixed = v * spw[...] + bsp[0:1, :]
    ya = _dot((u * mixed).astype(BF16), w_a[...])
    s = _dot(hn, w_in[:, OFF_S:OFF_S + SSM_WIDTH])
    ys = []
    for j in range(SLABS):
        lanes = slice(j * SLAB_STATE, (j + 1) * SLAB_STATE)
        s_j = s[:, j * LANES:(j + 1) * LANES]
        bu = _dot(s_j.astype(BF16), bmat[j])
        a_re, a_im = are[0:1, lanes], aim[0:1, lanes]
        p_re, p_im = h0re[:, lanes], h0im[:, lanes]
        n_re = a_re * p_re - a_im * p_im + bu[:, 0:SLAB_STATE]
        n_im = a_re * p_im + a_im * p_re + bu[:, SLAB_STATE:]
        hre_o[:, lanes] = n_re
        him_o[:, lanes] = n_im
        h = jnp.concatenate([n_re, n_im], axis=1).astype(BF16)
        ys.append(_dot(h, cmat[j]) + dsk[:, j * LANES:(j + 1) * LANES] * s_j)
    pb = _dot(jnp.concatenate(ys, axis=1).astype(BF16), w_b[...])
    yb = pb[:, :D_MODEL] * jax.nn.sigmoid(pb[:, D_MODEL:])
    ga = jax.nn.sigmoid(_dot(hn, w_in[:, OFF_GA:OFF_GA + D_MODEL]))
    gb = jax.nn.sigmoid(_dot(hn, w_in[:, OFF_GB:OFF_GB + D_MODEL]))
    merged = (ga * ya + gb * yb).astype(BF16)
    x1_ref[...] = x + _dot(merged, w_out[...])


def _mixer_sample(x, h0re, h0im, gmix, w_in, lng, lnb, spw, bsp, are, aim, bmat, cmat, dsk,
                  w_a, w_b, w_out):
    n, _, d = x.shape
    return pl.pallas_call(
        _mixer_sample_kernel,
        out_shape=(jax.ShapeDtypeStruct((n, d), F32),
                   jax.ShapeDtypeStruct((n, STATE_W), F32),
                   jax.ShapeDtypeStruct((n, STATE_W), F32),
                   jax.ShapeDtypeStruct((n, 1, d), F32)),
        compiler_params=pltpu.CompilerParams(vmem_limit_bytes=VMEM_LIMIT),
        name="mixer_sample",
    )(x, h0re, h0im, gmix, w_in, lng, lnb, spw, bsp, are, aim, bmat, cmat, dsk, w_a, w_b, w_out)


def _ffn_rows(x, gffn, w_gate, w_up, w_down, gfin):
    h = _rmsnorm(x, gffn[...]).astype(BF16)
    act = (jax.nn.silu(_dot(h, w_gate[...])) * _dot(h, w_up[...])).astype(BF16)
    x2 = x + _dot(act, w_down[...])
    return _rmsnorm(x2, gfin[...])


def _ffn_kernel(x_ref, xs_ref, gffn, wg_hbm, wu_hbm, wd_hbm, gfin, y_ref, ys_ref,
                w_gate, w_up, w_down, stage, sems):
    @pl.when(pl.program_id(0) == 0)
    def _():
        pieces = (_cast_pieces(wg_hbm, w_gate) + _cast_pieces(wu_hbm, w_up)
                  + _cast_pieces(wd_hbm, w_down))
        _load_cast(pieces, lambda slot, width: stage.at[slot, :, 0:width], sems)

    params = (gffn, w_gate, w_up, w_down, gfin)
    for r0 in range(0, x_ref.shape[0], FFN_SUB_ROWS):
        r = slice(r0, r0 + FFN_SUB_ROWS)
        y_ref[r, :] = _ffn_rows(x_ref[r, :], *params)

    @pl.when(pl.program_id(0) == pl.num_programs(0) - 1)
    def _():
        ys_ref[:, 0, :] = _ffn_rows(xs_ref[...], *params)


def _ffn(x, xs, gffn, w_gate, w_up, w_down, gfin, block_rows):
    n, d = x.shape
    ns = xs.shape[0]
    row_spec = pl.BlockSpec((block_rows, d), lambda i: (i, 0))
    hbm_spec = pl.BlockSpec(memory_space=pl.ANY)
    return pl.pallas_call(
        _ffn_kernel,
        grid=(n // block_rows,),
        in_specs=[row_spec, _const_spec(xs.shape), _const_spec(gffn.shape), hbm_spec, hbm_spec,
                  hbm_spec, _const_spec(gfin.shape)],
        out_specs=(row_spec, pl.BlockSpec((ns, 1, d), lambda i: (0, 0, 0))),
        out_shape=(jax.ShapeDtypeStruct((n, d), F32), jax.ShapeDtypeStruct((ns, 1, d), F32)),
        scratch_shapes=[pltpu.VMEM(w_gate.shape, BF16), pltpu.VMEM(w_up.shape, BF16),
                        pltpu.VMEM(w_down.shape, BF16),
                        pltpu.VMEM((STAGE_SLOTS, STAGE_ROWS, STAGE_COLS), F32),
                        pltpu.SemaphoreType.DMA((STAGE_SLOTS,))],
        compiler_params=pltpu.CompilerParams(dimension_semantics=("arbitrary",),
                                             vmem_limit_bytes=VMEM_LIMIT),
        name="ffn",
    )(x, xs, gffn, w_gate, w_up, w_down, gfin)


def kernel(x_prompt, x_sample, state_ssm_re, state_ssm_im, norm_mix_g, w_in, ln_v_g, ln_v_b,
           w_spatial, b_spatial, ssm_lam_re, ssm_lam_im, ssm_log_dt, ssm_b_re, ssm_b_im,
           ssm_c_re, ssm_c_im, ssm_d, w_branch_a, w_branch_b, w_out, norm_ffn_g,
           w_gate_ffn, w_up_ffn, w_down_ffn, norm_final_g):
    depth = w_in.shape[0]
    assert depth == 1
    nb, seq, d = x_prompt.shape
    ns = x_sample.shape[0]
    row = lambda a: a.reshape(1, -1)

    xp = x_prompt
    xs = x_sample
    gfin = row(norm_final_g)
    p_re, p_im, s_re, s_im, s_v = [], [], [], [], []
    for l in range(depth):
        are, aim, bmat, cmat, wsp, bsp, spw = _prep(
            ssm_lam_re[l], ssm_lam_im[l], ssm_log_dt[l], ssm_b_re[l], ssm_b_im[l],
            ssm_c_re[l], ssm_c_im[l], w_spatial[l], b_spatial[l])
        dsk = row(ssm_d[l])
        shared = dict(gmix=row(norm_mix_g[l]), w_in=w_in[l].astype(BF16), lng=row(ln_v_g[l]),
                      lnb=row(ln_v_b[l]))
        tail = dict(are=are, aim=aim, bmat=bmat, cmat=cmat, dsk=dsk,
                    w_a=w_branch_a[l].astype(BF16), w_b=w_branch_b[l].astype(BF16),
                    w_out=w_out[l].astype(BF16))
        ffn_w = (row(norm_ffn_g[l]), w_gate_ffn[l], w_up_ffn[l], w_down_ffn[l])

        xp, hre, him = _mixer_prompt(xp, **shared, wsp=wsp, bsp=bsp, **tail)
        p_re.append(hre.reshape(nb, SSM_GROUPS, SSM_STATE))
        p_im.append(him.reshape(nb, SSM_GROUPS, SSM_STATE))

        xs, hre, him, v_rows = _mixer_sample(
            xs, state_ssm_re[l].reshape(ns, STATE_W), state_ssm_im[l].reshape(ns, STATE_W),
            **shared, spw=spw, bsp=bsp, **tail)
        s_re.append(hre.reshape(ns, SSM_GROUPS, SSM_STATE))
        s_im.append(him.reshape(ns, SSM_GROUPS, SSM_STATE))
        s_v.append(v_rows)

        xp, xs = _ffn(xp.reshape(nb * seq, d), xs, *ffn_w, gfin, FFN_BLOCK_ROWS)
        xp = xp.reshape(nb, seq, d)

    return (xp, xs, jnp.stack(p_re), jnp.stack(p_im), jnp.stack(s_re),
            jnp.stack(s_im), jnp.stack(s_v))
```

```python
import jax
import jax.numpy as jnp
from jax import lax
from jax.experimental import pallas as pl
from jax.experimental.pallas import tpu as pltpu

F32 = jnp.float32
BF16 = jnp.bfloat16

D_MODEL = 1024
CHUNK = 128
GMLP_GROUPS = 8
GROUP_CH = D_MODEL // GMLP_GROUPS
SSM_WIDTH = 512
SSM_GROUPS = 32
SSM_STATE = 64
SSM_GROUP_CH = 16
LANES = 128
SUBLANES = 8
SLABS = SSM_WIDTH // LANES
SLAB_GROUPS = LANES // SSM_GROUP_CH
SLAB_STATE = SLAB_GROUPS * SSM_STATE
STATE_W = SSM_GROUPS * SSM_STATE
EPS = 1e-6

OFF_U, OFF_V, OFF_S = 0, D_MODEL, 2 * D_MODEL
OFF_GA = 2 * D_MODEL + SSM_WIDTH
OFF_GB = OFF_GA + D_MODEL

ROW_BLOCK = 256
SCAN_STEPS = 16
FFN_BLOCK_ROWS = 512
FFN_SUB_ROWS = 256
VMEM_LIMIT = 56 * 1024 * 1024


def _dot(a, b):
    return jnp.dot(a, b, preferred_element_type=F32)


def _rmsnorm(x, g):
    return x * lax.rsqrt(jnp.mean(x * x, axis=-1, keepdims=True) + EPS) * g


def _layernorm(x, g, b):
    mu = jnp.mean(x, axis=-1, keepdims=True)
    xc = x - mu
    var = jnp.mean(xc * xc, axis=-1, keepdims=True)
    return xc * lax.rsqrt(var + EPS) * g + b


def _const_spec(shape):
    zeros = (0,) * len(shape)
    return pl.BlockSpec(shape, lambda *_: zeros, pipeline_mode=pl.Buffered(1))


STAGE_SLOTS = 8
STAGE_ROWS, STAGE_COLS = 256, 1024


def _cast_pieces(src_hbm, dst_vmem):
    n_rows, n_cols = src_hbm.shape
    assert n_rows % STAGE_ROWS == 0 and n_cols % LANES == 0
    pieces = []
    for r in range(0, n_rows, STAGE_ROWS):
        for c in range(0, n_cols, STAGE_COLS):
            w = min(STAGE_COLS, n_cols - c)
            pieces.append((src_hbm.at[r:r + STAGE_ROWS, c:c + w],
                           dst_vmem.at[r:r + STAGE_ROWS, c:c + w], w))
    return pieces


def _load_cast(pieces, stage_view, sems):
    def copy(k):
        src, _, width = pieces[k]
        slot = k % STAGE_SLOTS
        return pltpu.make_async_copy(src, stage_view(slot, width), sems.at[slot])

    for k in range(min(STAGE_SLOTS, len(pieces))):
        copy(k).start()
    for k, (_, dst, width) in enumerate(pieces):
        copy(k).wait()
        dst[...] = stage_view(k % STAGE_SLOTS, width)[...].astype(BF16)
        if k + STAGE_SLOTS < len(pieces):
            copy(k + STAGE_SLOTS).start()


def _dot_exact(a, b):
    return jnp.dot(a, b, preferred_element_type=F32, precision=lax.Precision.HIGHEST)


def _group_mask(shape, row_div, lane_div):
    rows = lax.broadcasted_iota(jnp.int32, shape, 0) // row_div
    lanes = lax.broadcasted_iota(jnp.int32, shape, 1) // lane_div
    return rows == lanes


def _prep_kernel(lre, lim, ldt, bre, bim, cre_in, cim_in, wsp_in, bsp_in,
                 are_o, aim_o, bmat_o, cmat_o, wsp_o, bsp_o, spw_o):
    def slab_rows(row_of_group):
        slabs = [[row_of_group(j * SLAB_GROUPS + k) for k in range(SLAB_GROUPS)]
                 for j in range(SLABS)]
        return jnp.concatenate([jnp.concatenate(rows, axis=1) for rows in slabs], axis=0)

    lr = slab_rows(lambda g: lre[g:g + 1, :])
    li = slab_rows(lambda g: lim[g:g + 1, :])
    dt = jnp.exp(slab_rows(lambda g: jnp.broadcast_to(ldt[0:1, g:g + 1], (1, SSM_STATE))))
    mag = jnp.exp(lr * dt)
    a_re, a_im = mag * jnp.cos(li * dt), mag * jnp.sin(li * dt)
    n_re, n_im = a_re - 1.0, a_im
    den = lr * lr + li * li
    k_re = (n_re * lr + n_im * li) / den
    k_im = (n_im * lr - n_re * li) / den

    rep_h = (lax.broadcasted_iota(jnp.int32, (SSM_GROUP_CH, LANES), 1) % SSM_GROUP_CH
             == lax.broadcasted_iota(jnp.int32, (SSM_GROUP_CH, LANES), 0)).astype(F32)
    rep_p = (lax.broadcasted_iota(jnp.int32, (SSM_STATE, SLAB_STATE), 1) % SSM_STATE
             == lax.broadcasted_iota(jnp.int32, (SSM_STATE, SLAB_STATE), 0)).astype(F32)
    tb_re = _dot_exact(bre[...], rep_h)
    tb_im = _dot_exact(bim[...], rep_h)
    tc_re = _dot_exact(cre_in[...], rep_p)
    tc_im = _dot_exact(cim_in[...], rep_p)
    mask_b = _group_mask((SLAB_STATE, LANES), SSM_STATE, SSM_GROUP_CH)
    mask_c = _group_mask((LANES, SLAB_STATE), SSM_GROUP_CH, SSM_STATE)

    for j in range(SLABS):
        lanes = slice(j * SLAB_STATE, (j + 1) * SLAB_STATE)
        are_o[:, lanes] = jnp.broadcast_to(a_re[j:j + 1, :], (SUBLANES, SLAB_STATE))
        aim_o[:, lanes] = jnp.broadcast_to(a_im[j:j + 1, :], (SUBLANES, SLAB_STATE))
        rows = slice(j * SLAB_STATE, (j + 1) * SLAB_STATE)
        m_re = jnp.where(mask_b, tb_re[rows], 0.0).T
        m_im = jnp.where(mask_b, tb_im[rows], 0.0).T
        c_re, c_im = k_re[j:j + 1, :], k_im[j:j + 1, :]
        bmat_o[j] = jnp.concatenate([c_re * m_re - c_im * m_im, c_re * m_im + c_im * m_re],
                                    axis=1).astype(BF16)
        rows = slice(j * LANES, (j + 1) * LANES)
        read = jnp.concatenate([jnp.where(mask_c, tc_re[rows], 0.0),
                                jnp.where(mask_c, -tc_im[rows], 0.0)], axis=1)
        cmat_o[j] = read.T.astype(BF16)

    tril = (lax.broadcasted_iota(jnp.int32, (CHUNK, CHUNK), 0)
            >= lax.broadcasted_iota(jnp.int32, (CHUNK, CHUNK), 1))
    spw = []
    for g in range(GMLP_GROUPS):
        w = wsp_in[g]
        wsp_o[g] = jnp.where(tril, w, 0.0).astype(BF16)
        spw.append(jnp.broadcast_to(w[0:1, 0:1], (1, GROUP_CH)))
    spw_o[...] = jnp.concatenate(spw, axis=1)
    sel = _group_mask((D_MODEL, GMLP_GROUPS), GROUP_CH, 1).astype(F32)
    bsp_o[...] = _dot_exact(sel, bsp_in[...]).T


def _prep(lam_re, lam_im, log_dt, b_re, b_im, c_re, c_im, w_spatial, b_spatial):
    g, p, h = b_re.shape
    return pl.pallas_call(
        _prep_kernel,
        out_shape=(jax.ShapeDtypeStruct((SUBLANES, STATE_W), F32),
                   jax.ShapeDtypeStruct((SUBLANES, STATE_W), F32),
                   jax.ShapeDtypeStruct((SLABS, LANES, 2 * SLAB_STATE), BF16),
                   jax.ShapeDtypeStruct((SLABS, 2 * SLAB_STATE, LANES), BF16),
                   jax.ShapeDtypeStruct(w_spatial.shape, BF16),
                   jax.ShapeDtypeStruct((CHUNK, D_MODEL), F32),
                   jax.ShapeDtypeStruct((1, D_MODEL), F32)),
        name="prep",
    )(lam_re, lam_im, log_dt.reshape(1, g), b_re.reshape(g * p, h), b_im.reshape(g * p, h),
      c_re.reshape(g * h, p), c_im.reshape(g * h, p), w_spatial, b_spatial)


def _mixer_prompt_kernel(x_ref, gmix, w_in, lng, lnb, wsp, bsp, are, aim, bmat, cmat, dsk,
                         w_a, w_b, w_out,
                         x1_ref, hre_o, him_o,
                         s_tm, ys_tm, ya_s, hn_s, bu, hbuf, st_re, st_im):
    i = pl.program_id(0)
    nb = x_ref.shape[0]
    rows = nb * CHUNK
    per_block = ROW_BLOCK // CHUNK

    @pl.when(i == 0)
    def _():
        st_re[...] = jnp.zeros_like(st_re)
        st_im[...] = jnp.zeros_like(st_im)

    def project(blk):
        r = slice(blk * ROW_BLOCK, (blk + 1) * ROW_BLOCK)
        x = jnp.concatenate([x_ref[blk * per_block + k] for k in range(per_block)], axis=0)
        hn = _rmsnorm(x, gmix[...]).astype(BF16)
        hn_s[r, :] = hn
        zu = _dot(hn, w_in[:, OFF_U:OFF_U + D_MODEL])
        zv = _dot(hn, w_in[:, OFF_V:OFF_V + D_MODEL])
        s = _dot(hn, w_in[:, OFF_S:OFF_S + SSM_WIDTH])
        for k in range(per_block):
            b = blk * per_block + k
            for j in range(SLABS):
                s_tm[j, pl.ds(b, CHUNK, stride=nb), :] = s[k * CHUNK:(k + 1) * CHUNK,
                                                          j * LANES:(j + 1) * LANES]
        return zu, zv

    def gate(blk, zu, zv):
        u = jax.nn.gelu(zu)
        vb = _layernorm(jax.nn.gelu(zv), lng[...], lnb[...]).astype(BF16)
        wide = [_dot(wsp[g], jnp.concatenate(
                    [vb[k * CHUNK:(k + 1) * CHUNK, g * GROUP_CH:(g + 1) * GROUP_CH]
                     for k in range(per_block)], axis=1))
                for g in range(GMLP_GROUPS)]
        mixed = [jnp.concatenate([w[:, k * GROUP_CH:(k + 1) * GROUP_CH] for w in wide], axis=1)
                 + bsp[...] for k in range(per_block)]
        gated = (u * jnp.concatenate(mixed, axis=0)).astype(BF16)
        ya_s[blk * ROW_BLOCK:(blk + 1) * ROW_BLOCK, :] = _dot(gated, w_a[...])

    n_blk = rows // ROW_BLOCK
    z = project(0)
    for blk in range(n_blk):
        z_next = project(blk + 1) if blk + 1 < n_blk else None
        gate(blk, *z)
        z = z_next

    sub_rows = SCAN_STEPS * nb

    def project_in(q):
        for j in range(SLABS):
            s_blk = s_tm[j, q * sub_rows:(q + 1) * sub_rows, :]
            bu[q % 2, j] = _dot(s_blk.astype(BF16), bmat[j])

    def recur(q):
        for j in range(SLABS):
            lanes = slice(j * SLAB_STATE, (j + 1) * SLAB_STATE)
            a_re, a_im = are[:, lanes], aim[:, lanes]
            h_re, h_im = st_re[:, lanes], st_im[:, lanes]
            for t in range(0, SCAN_STEPS, 2):
                pair = []
                for r in (slice(t * nb, (t + 1) * nb), slice((t + 1) * nb, (t + 2) * nb)):
                    n_re = a_re * h_re - a_im * h_im + bu[q % 2, j, r, 0:SLAB_STATE]
                    n_im = a_re * h_im + a_im * h_re + bu[q % 2, j, r, SLAB_STATE:2 * SLAB_STATE]
                    h_re, h_im = n_re, n_im
                    pair.append(jnp.concatenate([n_re, n_im], axis=1))
                hbuf[q % 2, j, t * nb:(t + 2) * nb, :] = jnp.concatenate(pair, axis=0).astype(BF16)
            st_re[:, lanes] = h_re
            st_im[:, lanes] = h_im

    def project_out(q):
        for j in range(SLABS):
            r = slice(q * sub_rows, (q + 1) * sub_rows)
            d_j = dsk[:, j * LANES:(j + 1) * LANES]
            ys_tm[j, r, :] = _dot(hbuf[q % 2, j], cmat[j]) + d_j * s_tm[j, r, :]

    n_sub = rows // sub_rows
    project_in(0)
    for q in range(n_sub):
        if q + 1 < n_sub:
            project_in(q + 1)
        recur(q)
        if q >= 1:
            project_out(q - 1)
    project_out(n_sub - 1)

    def merge(blk):
        r = slice(blk * ROW_BLOCK, (blk + 1) * ROW_BLOCK)
        ys = jnp.concatenate(
            [jnp.concatenate([ys_tm[j, pl.ds(blk * per_block + k, CHUNK, stride=nb), :]
                              for j in range(SLABS)], axis=1)
             for k in range(per_block)], axis=0)
        pb = _dot(ys.astype(BF16), w_b[...])
        yb = pb[:, :D_MODEL] * jax.nn.sigmoid(pb[:, D_MODEL:])
        hn = hn_s[r, :]
        ga = jax.nn.sigmoid(_dot(hn, w_in[:, OFF_GA:OFF_GA + D_MODEL]))
        gb = jax.nn.sigmoid(_dot(hn, w_in[:, OFF_GB:OFF_GB + D_MODEL]))
        merged = (ga * ya_s[r, :] + gb * yb).astype(BF16)
        y = _dot(merged, w_out[...])
        for k in range(per_block):
            b = blk * per_block + k
            x1_ref[b] = x_ref[b] + y[k * CHUNK:(k + 1) * CHUNK]

    for blk in range(n_blk):
        merge(blk)

    @pl.when(i == pl.num_programs(0) - 1)
    def _():
        hre_o[...] = st_re[...]
        him_o[...] = st_im[...]


def _mixer_prompt(x, gmix, w_in, lng, lnb, wsp, bsp, are, aim, bmat, cmat, dsk, w_a, w_b, w_out):
    nb, seq, d = x.shape
    rows = nb * CHUNK
    consts = (gmix, w_in, lng, lnb, wsp, bsp, are, aim, bmat, cmat, dsk, w_a, w_b, w_out)
    x_spec = pl.BlockSpec((nb, CHUNK, d), lambda i: (0, i, 0))
    st_spec = pl.BlockSpec((nb, STATE_W), lambda i: (0, 0))
    return pl.pallas_call(
        _mixer_prompt_kernel,
        grid=(seq // CHUNK,),
        in_specs=[x_spec] + [_const_spec(c.shape) for c in consts],
        out_specs=(x_spec, st_spec, st_spec),
        out_shape=(jax.ShapeDtypeStruct(x.shape, F32),
                   jax.ShapeDtypeStruct((nb, STATE_W), F32),
                   jax.ShapeDtypeStruct((nb, STATE_W), F32)),
        scratch_shapes=[
            pltpu.VMEM((SLABS, rows, LANES), F32),
            pltpu.VMEM((SLABS, rows, LANES), F32),
            pltpu.VMEM((rows, d), F32),
            pltpu.VMEM((rows, d), BF16),
            pltpu.VMEM((2, SLABS, SCAN_STEPS * nb, 2 * SLAB_STATE), F32),
            pltpu.VMEM((2, SLABS, SCAN_STEPS * nb, 2 * SLAB_STATE), BF16),
            pltpu.VMEM((nb, STATE_W), F32),
            pltpu.VMEM((nb, STATE_W), F32),
        ],
        compiler_params=pltpu.CompilerParams(dimension_semantics=("arbitrary",),
                                             vmem_limit_bytes=VMEM_LIMIT),
        name="mixer_prompt",
    )(x, *consts)


def _mixer_sample_kernel(x_ref, h0re, h0im, gmix, w_in_hbm, lng, lnb, spw, bsp, are, aim, bmat,
                         cmat, dsk, w_a_hbm, w_b_hbm, w_out_hbm,
                         x1_ref, hre_o, him_o, v_o,
                         w_in, w_a, w_b, w_out, sems):
    first, rest = slice(0, OFF_GA), slice(OFF_GA, OFF_GB + D_MODEL)
    pairs = [(w_in_hbm.at[:, first], w_in.at[:, first]), (w_a_hbm, w_a), (w_b_hbm, w_b),
             (w_in_hbm.at[:, rest], w_in.at[:, rest]), (w_out_hbm, w_out)]
    copies = [pltpu.make_async_copy(src, dst, sems.at[k]) for k, (src, dst) in enumerate(pairs)]
    for c in copies:
        c.start()
    x = x_ref[:, 0, :]
    hn = _rmsnorm(x, gmix[...]).astype(BF16)
    copies[0].wait()
    u = jax.nn.gelu(_dot(hn, w_in[:, OFF_U:OFF_U + D_MODEL]))
    v = _layernorm(jax.nn.gelu(_dot(hn, w_in[:, OFF_V:OFF_V + D_MODEL])), lng[...], lnb[...])
    v_o[:, 0, :] = v
    s = _dot(hn, w_in[:, OFF_S:OFF_S + SSM_WIDTH])
    mixed = v * spw[...] + bsp[0:1, :]
    copies[1].wait()
    ya = _dot((u * mixed).astype(BF16), w_a[...])
    ys = []
    for j in range(SLABS):
        lanes = slice(j * SLAB_STATE, (j + 1) * SLAB_STATE)
        s_j = s[:, j * LANES:(j + 1) * LANES]
        bu = _dot(s_j.astype(BF16), bmat[j])
        a_re, a_im = are[0:1, lanes], aim[0:1, lanes]
        p_re, p_im = h0re[:, lanes], h0im[:, lanes]
        n_re = a_re * p_re - a_im * p_im + bu[:, 0:SLAB_STATE]
        n_im = a_re * p_im + a_im * p_re + bu[:, SLAB_STATE:]
        hre_o[:, lanes] = n_re
        him_o[:, lanes] = n_im
        h = jnp.concatenate([n_re, n_im], axis=1).astype(BF16)
        ys.append(_dot(h, cmat[j]) + dsk[:, j * LANES:(j + 1) * LANES] * s_j)
    copies[2].wait()
    pb = _dot(jnp.concatenate(ys, axis=1).astype(BF16), w_b[...])
    yb = pb[:, :D_MODEL] * jax.nn.sigmoid(pb[:, D_MODEL:])
    copies[3].wait()
    ga = jax.nn.sigmoid(_dot(hn, w_in[:, OFF_GA:OFF_GA + D_MODEL]))
    gb = jax.nn.sigmoid(_dot(hn, w_in[:, OFF_GB:OFF_GB + D_MODEL]))
    merged = (ga * ya + gb * yb).astype(BF16)
    copies[4].wait()
    x1_ref[...] = x + _dot(merged, w_out[...])


def _mixer_sample(x, h0re, h0im, gmix, w_in, lng, lnb, spw, bsp, are, aim, bmat, cmat, dsk,
                  w_a, w_b, w_out):
    n, _, d = x.shape
    hbm = pl.BlockSpec(memory_space=pl.ANY)
    vmem = pl.BlockSpec(memory_space=pltpu.VMEM)
    return pl.pallas_call(
        _mixer_sample_kernel,
        in_specs=[vmem, vmem, vmem, vmem, hbm, vmem, vmem, vmem, vmem, vmem, vmem, vmem, vmem,
                  vmem, hbm, hbm, hbm],
        out_shape=(jax.ShapeDtypeStruct((n, d), F32),
                   jax.ShapeDtypeStruct((n, STATE_W), F32),
                   jax.ShapeDtypeStruct((n, STATE_W), F32),
                   jax.ShapeDtypeStruct((n, 1, d), F32)),
        scratch_shapes=[pltpu.VMEM(w_in.shape, BF16), pltpu.VMEM(w_a.shape, BF16),
                        pltpu.VMEM(w_b.shape, BF16), pltpu.VMEM(w_out.shape, BF16),
                        pltpu.SemaphoreType.DMA((5,))],
        compiler_params=pltpu.CompilerParams(vmem_limit_bytes=VMEM_LIMIT),
        name="mixer_sample",
    )(x, h0re, h0im, gmix, w_in, lng, lnb, spw, bsp, are, aim, bmat, cmat, dsk, w_a, w_b, w_out)


def _ffn_rows(x, gffn, w_gate, w_up, w_down, gfin):
    h = _rmsnorm(x, gffn[...]).astype(BF16)
    act = (jax.nn.silu(_dot(h, w_gate[...])) * _dot(h, w_up[...])).astype(BF16)
    x2 = x + _dot(act, w_down[...])
    return _rmsnorm(x2, gfin[...])


def _ffn_kernel(x_ref, xs_ref, gffn, wg_hbm, wu_hbm, wd_hbm, gfin, y_ref, ys_ref,
                w_gate, w_up, w_down, stage, sems):
    @pl.when(pl.program_id(0) == 0)
    def _():
        pieces = (_cast_pieces(wg_hbm, w_gate) + _cast_pieces(wu_hbm, w_up)
                  + _cast_pieces(wd_hbm, w_down))
        _load_cast(pieces, lambda slot, width: stage.at[slot, :, 0:width], sems)

    params = (gffn, w_gate, w_up, w_down, gfin)
    for r0 in range(0, x_ref.shape[0], FFN_SUB_ROWS):
        r = slice(r0, r0 + FFN_SUB_ROWS)
        y_ref[r, :] = _ffn_rows(x_ref[r, :], *params)

    @pl.when(pl.program_id(0) == pl.num_programs(0) - 1)
    def _():
        ys_ref[:, 0, :] = _ffn_rows(xs_ref[...], *params)


def _ffn(x, xs, gffn, w_gate, w_up, w_down, gfin, block_rows):
    n, d = x.shape
    ns = xs.shape[0]
    row_spec = pl.BlockSpec((block_rows, d), lambda i: (i, 0))
    hbm_spec = pl.BlockSpec(memory_space=pl.ANY)
    return pl.pallas_call(
        _ffn_kernel,
        grid=(n // block_rows,),
        in_specs=[row_spec, _const_spec(xs.shape), _const_spec(gffn.shape), hbm_spec, hbm_spec,
                  hbm_spec, _const_spec(gfin.shape)],
        out_specs=(row_spec, pl.BlockSpec((ns, 1, d), lambda i: (0, 0, 0))),
        out_shape=(jax.ShapeDtypeStruct((n, d), F32), jax.ShapeDtypeStruct((ns, 1, d), F32)),
        scratch_shapes=[pltpu.VMEM(w_gate.shape, BF16), pltpu.VMEM(w_up.shape, BF16),
                        pltpu.VMEM(w_down.shape, BF16),
                        pltpu.VMEM((STAGE_SLOTS, STAGE_ROWS, STAGE_COLS), F32),
                        pltpu.SemaphoreType.DMA((STAGE_SLOTS,))],
        compiler_params=pltpu.CompilerParams(dimension_semantics=("arbitrary",),
                                             vmem_limit_bytes=VMEM_LIMIT),
        name="ffn",
    )(x, xs, gffn, w_gate, w_up, w_down, gfin)


def kernel(x_prompt, x_sample, state_ssm_re, state_ssm_im, norm_mix_g, w_in, ln_v_g, ln_v_b,
           w_spatial, b_spatial, ssm_lam_re, ssm_lam_im, ssm_log_dt, ssm_b_re, ssm_b_im,
           ssm_c_re, ssm_c_im, ssm_d, w_branch_a, w_branch_b, w_out, norm_ffn_g,
           w_gate_ffn, w_up_ffn, w_down_ffn, norm_final_g):
    depth = w_in.shape[0]
    assert depth == 1
    nb, seq, d = x_prompt.shape
    ns = x_sample.shape[0]
    row = lambda a: a.reshape(1, -1)

    xp = x_prompt
    xs = x_sample
    gfin = row(norm_final_g)
    p_re, p_im, s_re, s_im, s_v = [], [], [], [], []
    for l in range(depth):
        are, aim, bmat, cmat, wsp, bsp, spw = _prep(
            ssm_lam_re[l], ssm_lam_im[l], ssm_log_dt[l], ssm_b_re[l], ssm_b_im[l],
            ssm_c_re[l], ssm_c_im[l], w_spatial[l], b_spatial[l])
        dsk = row(ssm_d[l])
        shared = dict(gmix=row(norm_mix_g[l]), w_in=w_in[l].astype(BF16), lng=row(ln_v_g[l]),
                      lnb=row(ln_v_b[l]))
        tail = dict(are=are, aim=aim, bmat=bmat, cmat=cmat, dsk=dsk,
                    w_a=w_branch_a[l].astype(BF16), w_b=w_branch_b[l].astype(BF16),
                    w_out=w_out[l].astype(BF16))
        ffn_w = (row(norm_ffn_g[l]), w_gate_ffn[l], w_up_ffn[l], w_down_ffn[l])

        xp, hre, him = _mixer_prompt(xp, **shared, wsp=wsp, bsp=bsp, **tail)
        p_re.append(hre.reshape(nb, SSM_GROUPS, SSM_STATE))
        p_im.append(him.reshape(nb, SSM_GROUPS, SSM_STATE))

        xs, hre, him, v_rows = _mixer_sample(
            xs, state_ssm_re[l].reshape(ns, STATE_W), state_ssm_im[l].reshape(ns, STATE_W),
            **shared, spw=spw, bsp=bsp, **tail)
        s_re.append(hre.reshape(ns, SSM_GROUPS, SSM_STATE))
        s_im.append(him.reshape(ns, SSM_GROUPS, SSM_STATE))
        s_v.append(v_rows)

        xp, xs = _ffn(xp.reshape(nb * seq, d), xs, *ffn_w, gfin, FFN_BLOCK_ROWS)
        xp = xp.reshape(nb, seq, d)

    return (xp, xs, jnp.stack(p_re), jnp.stack(p_im), jnp.stack(s_re),
            jnp.stack(s_im), jnp.stack(s_v))
```

```python
import jax
import jax.numpy as jnp
from jax import lax
from jax.experimental import pallas as pl
from jax.experimental.pallas import tpu as pltpu

F32 = jnp.float32
BF16 = jnp.bfloat16

D_MODEL = 1024
CHUNK = 128
GMLP_GROUPS = 8
GROUP_CH = D_MODEL // GMLP_GROUPS
SSM_WIDTH = 512
SSM_GROUPS = 32
SSM_STATE = 64
SSM_GROUP_CH = 16
LANES = 128
SUBLANES = 8
SLABS = SSM_WIDTH // LANES
SLAB_GROUPS = LANES // SSM_GROUP_CH
SLAB_STATE = SLAB_GROUPS * SSM_STATE
STATE_W = SSM_GROUPS * SSM_STATE
EPS = 1e-6

OFF_U, OFF_V, OFF_S = 0, D_MODEL, 2 * D_MODEL
OFF_GA = 2 * D_MODEL + SSM_WIDTH
OFF_GB = OFF_GA + D_MODEL

ROW_BLOCK = 256
SCAN_STEPS = 16
FFN_BLOCK_ROWS = 512
FFN_SUB_ROWS = 256
VMEM_LIMIT = 56 * 1024 * 1024


def _dot(a, b):
    return jnp.dot(a, b, preferred_element_type=F32)


def _rmsnorm(x, g):
    return x * lax.rsqrt(jnp.mean(x * x, axis=-1, keepdims=True) + EPS) * g


def _layernorm(x, g, b):
    mu = jnp.mean(x, axis=-1, keepdims=True)
    xc = x - mu
    var = jnp.mean(xc * xc, axis=-1, keepdims=True)
    return xc * lax.rsqrt(var + EPS) * g + b


def _const_spec(shape):
    zeros = (0,) * len(shape)
    return pl.BlockSpec(shape, lambda *_: zeros, pipeline_mode=pl.Buffered(1))


STAGE_SLOTS = 16
STAGE_ROWS, STAGE_COLS = 256, 1024


def _cast_pieces(src_hbm, dst_vmem):
    n_rows, n_cols = src_hbm.shape
    assert n_rows % STAGE_ROWS == 0 and n_cols % LANES == 0
    pieces = []
    for r in range(0, n_rows, STAGE_ROWS):
        for c in range(0, n_cols, STAGE_COLS):
            w = min(STAGE_COLS, n_cols - c)
            pieces.append((src_hbm.at[r:r + STAGE_ROWS, c:c + w],
                           dst_vmem.at[r:r + STAGE_ROWS, c:c + w], w))
    return pieces


def _load_cast(pieces, stage_view, sems):
    def copy(k):
        src, _, width = pieces[k]
        slot = k % STAGE_SLOTS
        return pltpu.make_async_copy(src, stage_view(slot, width), sems.at[slot])

    for k in range(min(STAGE_SLOTS, len(pieces))):
        copy(k).start()
    for k, (_, dst, width) in enumerate(pieces):
        copy(k).wait()
        dst[...] = stage_view(k % STAGE_SLOTS, width)[...].astype(BF16)
        if k + STAGE_SLOTS < len(pieces):
            copy(k + STAGE_SLOTS).start()


def _dot_exact(a, b):
    return jnp.dot(a, b, preferred_element_type=F32, precision=lax.Precision.HIGHEST)


def _group_mask(shape, row_div, lane_div):
    rows = lax.broadcasted_iota(jnp.int32, shape, 0) // row_div
    lanes = lax.broadcasted_iota(jnp.int32, shape, 1) // lane_div
    return rows == lanes


def _prep_kernel(lre, lim, ldt, bre, bim, cre_in, cim_in, wsp_in, bsp_in,
                 are_o, aim_o, bmat_o, cmat_o, wsp_o, bsp_o, spw_o):
    def slab_rows(row_of_group):
        slabs = [[row_of_group(j * SLAB_GROUPS + k) for k in range(SLAB_GROUPS)]
                 for j in range(SLABS)]
        return jnp.concatenate([jnp.concatenate(rows, axis=1) for rows in slabs], axis=0)

    lr = slab_rows(lambda g: lre[g:g + 1, :])
    li = slab_rows(lambda g: lim[g:g + 1, :])
    dt = jnp.exp(slab_rows(lambda g: jnp.broadcast_to(ldt[0:1, g:g + 1], (1, SSM_STATE))))
    mag = jnp.exp(lr * dt)
    a_re, a_im = mag * jnp.cos(li * dt), mag * jnp.sin(li * dt)
    n_re, n_im = a_re - 1.0, a_im
    den = lr * lr + li * li
    k_re = (n_re * lr + n_im * li) / den
    k_im = (n_im * lr - n_re * li) / den

    rep_h = (lax.broadcasted_iota(jnp.int32, (SSM_GROUP_CH, LANES), 1) % SSM_GROUP_CH
             == lax.broadcasted_iota(jnp.int32, (SSM_GROUP_CH, LANES), 0)).astype(F32)
    rep_p = (lax.broadcasted_iota(jnp.int32, (SSM_STATE, SLAB_STATE), 1) % SSM_STATE
             == lax.broadcasted_iota(jnp.int32, (SSM_STATE, SLAB_STATE), 0)).astype(F32)
    tb_re = _dot_exact(bre[...], rep_h)
    tb_im = _dot_exact(bim[...], rep_h)
    tc_re = _dot_exact(cre_in[...], rep_p)
    tc_im = _dot_exact(cim_in[...], rep_p)
    mask_b = _group_mask((SLAB_STATE, LANES), SSM_STATE, SSM_GROUP_CH)
    mask_c = _group_mask((LANES, SLAB_STATE), SSM_GROUP_CH, SSM_STATE)

    for j in range(SLABS):
        lanes = slice(j * SLAB_STATE, (j + 1) * SLAB_STATE)
        are_o[:, lanes] = jnp.broadcast_to(a_re[j:j + 1, :], (SUBLANES, SLAB_STATE))
        aim_o[:, lanes] = jnp.broadcast_to(a_im[j:j + 1, :], (SUBLANES, SLAB_STATE))
        rows = slice(j * SLAB_STATE, (j + 1) * SLAB_STATE)
        m_re = jnp.where(mask_b, tb_re[rows], 0.0).T
        m_im = jnp.where(mask_b, tb_im[rows], 0.0).T
        c_re, c_im = k_re[j:j + 1, :], k_im[j:j + 1, :]
        bmat_o[j] = jnp.concatenate([c_re * m_re - c_im * m_im, c_re * m_im + c_im * m_re],
                                    axis=1).astype(BF16)
        rows = slice(j * LANES, (j + 1) * LANES)
        read = jnp.concatenate([jnp.where(mask_c, tc_re[rows], 0.0),
                                jnp.where(mask_c, -tc_im[rows], 0.0)], axis=1)
        cmat_o[j] = read.T.astype(BF16)

    tril = (lax.broadcasted_iota(jnp.int32, (CHUNK, CHUNK), 0)
            >= lax.broadcasted_iota(jnp.int32, (CHUNK, CHUNK), 1))
    spw = []
    for g in range(GMLP_GROUPS):
        w = wsp_in[g]
        wsp_o[g] = jnp.where(tril, w, 0.0).astype(BF16)
        spw.append(jnp.broadcast_to(w[0:1, 0:1], (1, GROUP_CH)))
    spw_o[...] = jnp.concatenate(spw, axis=1)
    sel = _group_mask((D_MODEL, GMLP_GROUPS), GROUP_CH, 1).astype(F32)
    bsp_o[...] = _dot_exact(sel, bsp_in[...]).T


def _prep(lam_re, lam_im, log_dt, b_re, b_im, c_re, c_im, w_spatial, b_spatial):
    g, p, h = b_re.shape
    return pl.pallas_call(
        _prep_kernel,
        out_shape=(jax.ShapeDtypeStruct((SUBLANES, STATE_W), F32),
                   jax.ShapeDtypeStruct((SUBLANES, STATE_W), F32),
                   jax.ShapeDtypeStruct((SLABS, LANES, 2 * SLAB_STATE), BF16),
                   jax.ShapeDtypeStruct((SLABS, 2 * SLAB_STATE, LANES), BF16),
                   jax.ShapeDtypeStruct(w_spatial.shape, BF16),
                   jax.ShapeDtypeStruct((CHUNK, D_MODEL), F32),
                   jax.ShapeDtypeStruct((1, D_MODEL), F32)),
        name="prep",
    )(lam_re, lam_im, log_dt.reshape(1, g), b_re.reshape(g * p, h), b_im.reshape(g * p, h),
      c_re.reshape(g * h, p), c_im.reshape(g * h, p), w_spatial, b_spatial)


def _mixer_prompt_kernel(x_ref, gmix, w_in, lng, lnb, wsp, bsp, are, aim, bmat, cmat, dsk,
                         w_a, w_b, w_out,
                         x1_ref, hre_o, him_o,
                         s_tm, ys_tm, ya_s, hn_s, bu, hbuf, st_re, st_im):
    i = pl.program_id(0)
    nb = x_ref.shape[0]
    rows = nb * CHUNK
    per_block = ROW_BLOCK // CHUNK

    @pl.when(i == 0)
    def _():
        st_re[...] = jnp.zeros_like(st_re)
        st_im[...] = jnp.zeros_like(st_im)

    def project(blk):
        r = slice(blk * ROW_BLOCK, (blk + 1) * ROW_BLOCK)
        x = jnp.concatenate([x_ref[blk * per_block + k] for k in range(per_block)], axis=0)
        hn = _rmsnorm(x, gmix[...]).astype(BF16)
        hn_s[r, :] = hn
        zu = _dot(hn, w_in[:, OFF_U:OFF_U + D_MODEL])
        zv = _dot(hn, w_in[:, OFF_V:OFF_V + D_MODEL])
        s = _dot(hn, w_in[:, OFF_S:OFF_S + SSM_WIDTH])
        for k in range(per_block):
            b = blk * per_block + k
            for j in range(SLABS):
                s_tm[j, pl.ds(b, CHUNK, stride=nb), :] = s[k * CHUNK:(k + 1) * CHUNK,
                                                          j * LANES:(j + 1) * LANES]
        return zu, zv

    def gate(blk, zu, zv):
        u = jax.nn.gelu(zu)
        vb = _layernorm(jax.nn.gelu(zv), lng[...], lnb[...]).astype(BF16)
        wide = [_dot(wsp[g], jnp.concatenate(
                    [vb[k * CHUNK:(k + 1) * CHUNK, g * GROUP_CH:(g + 1) * GROUP_CH]
                     for k in range(per_block)], axis=1))
                for g in range(GMLP_GROUPS)]
        mixed = [jnp.concatenate([w[:, k * GROUP_CH:(k + 1) * GROUP_CH] for w in wide], axis=1)
                 + bsp[...] for k in range(per_block)]
        gated = (u * jnp.concatenate(mixed, axis=0)).astype(BF16)
        ya_s[blk * ROW_BLOCK:(blk + 1) * ROW_BLOCK, :] = _dot(gated, w_a[...])

    n_blk = rows // ROW_BLOCK
    z = project(0)
    for blk in range(n_blk):
        z_next = project(blk + 1) if blk + 1 < n_blk else None
        gate(blk, *z)
        z = z_next

    sub_rows = SCAN_STEPS * nb

    def project_in(q):
        for j in range(SLABS):
            s_blk = s_tm[j, q * sub_rows:(q + 1) * sub_rows, :]
            bu[q % 2, j] = _dot(s_blk.astype(BF16), bmat[j])

    def recur(q):
        for j in range(SLABS):
            lanes = slice(j * SLAB_STATE, (j + 1) * SLAB_STATE)
            a_re, a_im = are[:, lanes], aim[:, lanes]
            h_re, h_im = st_re[:, lanes], st_im[:, lanes]
            for t in range(0, SCAN_STEPS, 2):
                pair = []
                for r in (slice(t * nb, (t + 1) * nb), slice((t + 1) * nb, (t + 2) * nb)):
                    n_re = a_re * h_re - a_im * h_im + bu[q % 2, j, r, 0:SLAB_STATE]
                    n_im = a_re * h_im + a_im * h_re + bu[q % 2, j, r, SLAB_STATE:2 * SLAB_STATE]
                    h_re, h_im = n_re, n_im
                    pair.append(jnp.concatenate([n_re, n_im], axis=1))
                hbuf[q % 2, j, t * nb:(t + 2) * nb, :] = jnp.concatenate(pair, axis=0).astype(BF16)
            st_re[:, lanes] = h_re
            st_im[:, lanes] = h_im

    def project_out(q):
        for j in range(SLABS):
            r = slice(q * sub_rows, (q + 1) * sub_rows)
            d_j = dsk[:, j * LANES:(j + 1) * LANES]
            ys_tm[j, r, :] = _dot(hbuf[q % 2, j], cmat[j]) + d_j * s_tm[j, r, :]

    n_sub = rows // sub_rows
    project_in(0)
    for q in range(n_sub):
        if q + 1 < n_sub:
            project_in(q + 1)
        recur(q)
        if q >= 1:
            project_out(q - 1)
    project_out(n_sub - 1)

    def merge(blk):
        r = slice(blk * ROW_BLOCK, (blk + 1) * ROW_BLOCK)
        ys = jnp.concatenate(
            [jnp.concatenate([ys_tm[j, pl.ds(blk * per_block + k, CHUNK, stride=nb), :]
                              for j in range(SLABS)], axis=1)
             for k in range(per_block)], axis=0)
        pb = _dot(ys.astype(BF16), w_b[...])
        yb = pb[:, :D_MODEL] * jax.nn.sigmoid(pb[:, D_MODEL:])
        hn = hn_s[r, :]
        ga = jax.nn.sigmoid(_dot(hn, w_in[:, OFF_GA:OFF_GA + D_MODEL]))
        gb = jax.nn.sigmoid(_dot(hn, w_in[:, OFF_GB:OFF_GB + D_MODEL]))
        merged = (ga * ya_s[r, :] + gb * yb).astype(BF16)
        y = _dot(merged, w_out[...])
        for k in range(per_block):
            b = blk * per_block + k
            x1_ref[b] = x_ref[b] + y[k * CHUNK:(k + 1) * CHUNK]

    for blk in range(n_blk):
        merge(blk)

    @pl.when(i == pl.num_programs(0) - 1)
    def _():
        hre_o[...] = st_re[...]
        him_o[...] = st_im[...]


def _mixer_prompt(x, gmix, w_in, lng, lnb, wsp, bsp, are, aim, bmat, cmat, dsk, w_a, w_b, w_out):
    nb, seq, d = x.shape
    rows = nb * CHUNK
    consts = (gmix, w_in, lng, lnb, wsp, bsp, are, aim, bmat, cmat, dsk, w_a, w_b, w_out)
    x_spec = pl.BlockSpec((nb, CHUNK, d), lambda i: (0, i, 0))
    st_spec = pl.BlockSpec((nb, STATE_W), lambda i: (0, 0))
    return pl.pallas_call(
        _mixer_prompt_kernel,
        grid=(seq // CHUNK,),
        in_specs=[x_spec] + [_const_spec(c.shape) for c in consts],
        out_specs=(x_spec, st_spec, st_spec),
        out_shape=(jax.ShapeDtypeStruct(x.shape, F32),
                   jax.ShapeDtypeStruct((nb, STATE_W), F32),
                   jax.ShapeDtypeStruct((nb, STATE_W), F32)),
        scratch_shapes=[
            pltpu.VMEM((SLABS, rows, LANES), F32),
            pltpu.VMEM((SLABS, rows, LANES), F32),
            pltpu.VMEM((rows, d), F32),
            pltpu.VMEM((rows, d), BF16),
            pltpu.VMEM((2, SLABS, SCAN_STEPS * nb, 2 * SLAB_STATE), F32),
            pltpu.VMEM((2, SLABS, SCAN_STEPS * nb, 2 * SLAB_STATE), BF16),
            pltpu.VMEM((nb, STATE_W), F32),
            pltpu.VMEM((nb, STATE_W), F32),
        ],
        compiler_params=pltpu.CompilerParams(dimension_semantics=("arbitrary",),
                                             vmem_limit_bytes=VMEM_LIMIT),
        name="mixer_prompt",
    )(x, *consts)


def _mixer_sample_kernel(x_ref, h0re, h0im, gmix, w_in, lng, lnb, spw, bsp, are, aim, bmat, cmat,
                         dsk, w_a, w_b, w_out,
                         x1_ref, hre_o, him_o, v_o):
    x = x_ref[:, 0, :]
    hn = _rmsnorm(x, gmix[...]).astype(BF16)
    u = jax.nn.gelu(_dot(hn, w_in[:, OFF_U:OFF_U + D_MODEL]))
    v = _layernorm(jax.nn.gelu(_dot(hn, w_in[:, OFF_V:OFF_V + D_MODEL])), lng[...], lnb[...])
    v_o[:, 0, :] = v
    mixed = v * spw[...] + bsp[0:1, :]
    ya = _dot((u * mixed).astype(BF16), w_a[...])
    s = _dot(hn, w_in[:, OFF_S:OFF_S + SSM_WIDTH])
    ys = []
    for j in range(SLABS):
        lanes = slice(j * SLAB_STATE, (j + 1) * SLAB_STATE)
        s_j = s[:, j * LANES:(j + 1) * LANES]
        bu = _dot(s_j.astype(BF16), bmat[j])
        a_re, a_im = are[0:1, lanes], aim[0:1, lanes]
        p_re, p_im = h0re[:, lanes], h0im[:, lanes]
        n_re = a_re * p_re - a_im * p_im + bu[:, 0:SLAB_STATE]
        n_im = a_re * p_im + a_im * p_re + bu[:, SLAB_STATE:]
        hre_o[:, lanes] = n_re
        him_o[:, lanes] = n_im
        h = jnp.concatenate([n_re, n_im], axis=1).astype(BF16)
        ys.append(_dot(h, cmat[j]) + dsk[:, j * LANES:(j + 1) * LANES] * s_j)
    pb = _dot(jnp.concatenate(ys, axis=1).astype(BF16), w_b[...])
    yb = pb[:, :D_MODEL] * jax.nn.sigmoid(pb[:, D_MODEL:])
    ga = jax.nn.sigmoid(_dot(hn, w_in[:, OFF_GA:OFF_GA + D_MODEL]))
    gb = jax.nn.sigmoid(_dot(hn, w_in[:, OFF_GB:OFF_GB + D_MODEL]))
    merged = (ga * ya + gb * yb).astype(BF16)
    x1_ref[...] = x + _dot(merged, w_out[...])


def _mixer_sample(x, h0re, h0im, gmix, w_in, lng, lnb, spw, bsp, are, aim, bmat, cmat, dsk,
                  w_a, w_b, w_out):
    n, _, d = x.shape
    return pl.pallas_call(
        _mixer_sample_kernel,
        out_shape=(jax.ShapeDtypeStruct((n, d), F32),
                   jax.ShapeDtypeStruct((n, STATE_W), F32),
                   jax.ShapeDtypeStruct((n, STATE_W), F32),
                   jax.ShapeDtypeStruct((n, 1, d), F32)),
        compiler_params=pltpu.CompilerParams(vmem_limit_bytes=VMEM_LIMIT),
        name="mixer_sample",
    )(x, h0re, h0im, gmix, w_in, lng, lnb, spw, bsp, are, aim, bmat, cmat, dsk, w_a, w_b, w_out)


def _ffn_rows(x, gffn, w_gate, w_up, w_down, gfin):
    h = _rmsnorm(x, gffn[...]).astype(BF16)
    act = (jax.nn.silu(_dot(h, w_gate[...])) * _dot(h, w_up[...])).astype(BF16)
    x2 = x + _dot(act, w_down[...])
    return _rmsnorm(x2, gfin[...])


def _ffn_kernel(x_ref, xs_ref, gffn, wg_hbm, wu_hbm, wd_hbm, gfin, y_ref, ys_ref,
                w_gate, w_up, w_down, stage, sems):
    @pl.when(pl.program_id(0) == 0)
    def _():
        pieces = (_cast_pieces(wg_hbm, w_gate) + _cast_pieces(wu_hbm, w_up)
                  + _cast_pieces(wd_hbm, w_down))
        _load_cast(pieces, lambda slot, width: stage.at[slot, :, 0:width], sems)

    params = (gffn, w_gate, w_up, w_down, gfin)
    for r0 in range(0, x_ref.shape[0], FFN_SUB_ROWS):
        r = slice(r0, r0 + FFN_SUB_ROWS)
        y_ref[r, :] = _ffn_rows(x_ref[r, :], *params)

    @pl.when(pl.program_id(0) == pl.num_programs(0) - 1)
    def _():
        ys_ref[:, 0, :] = _ffn_rows(xs_ref[...], *params)


def _ffn(x, xs, gffn, w_gate, w_up, w_down, gfin, block_rows):
    n, d = x.shape
    ns = xs.shape[0]
    row_spec = pl.BlockSpec((block_rows, d), lambda i: (i, 0))
    hbm_spec = pl.BlockSpec(memory_space=pl.ANY)
    return pl.pallas_call(
        _ffn_kernel,
        grid=(n // block_rows,),
        in_specs=[row_spec, _const_spec(xs.shape), _const_spec(gffn.shape), hbm_spec, hbm_spec,
                  hbm_spec, _const_spec(gfin.shape)],
        out_specs=(row_spec, pl.BlockSpec((ns, 1, d), lambda i: (0, 0, 0))),
        out_shape=(jax.ShapeDtypeStruct((n, d), F32), jax.ShapeDtypeStruct((ns, 1, d), F32)),
        scratch_shapes=[pltpu.VMEM(w_gate.shape, BF16), pltpu.VMEM(w_up.shape, BF16),
                        pltpu.VMEM(w_down.shape, BF16),
                        pltpu.VMEM((STAGE_SLOTS, STAGE_ROWS, STAGE_COLS), F32),
                        pltpu.SemaphoreType.DMA((STAGE_SLOTS,))],
        compiler_params=pltpu.CompilerParams(dimension_semantics=("arbitrary",),
                                             vmem_limit_bytes=VMEM_LIMIT),
        name="ffn",
    )(x, xs, gffn, w_gate, w_up, w_down, gfin)


def kernel(x_prompt, x_sample, state_ssm_re, state_ssm_im, norm_mix_g, w_in, ln_v_g, ln_v_b,
           w_spatial, b_spatial, ssm_lam_re, ssm_lam_im, ssm_log_dt, ssm_b_re, ssm_b_im,
           ssm_c_re, ssm_c_im, ssm_d, w_branch_a, w_branch_b, w_out, norm_ffn_g,
           w_gate_ffn, w_up_ffn, w_down_ffn, norm_final_g):
    depth = w_in.shape[0]
    assert depth == 1
    nb, seq, d = x_prompt.shape
    ns = x_sample.shape[0]
    row = lambda a: a.reshape(1, -1)

    xp = x_prompt
    xs = x_sample
    gfin = row(norm_final_g)
    p_re, p_im, s_re, s_im, s_v = [], [], [], [], []
    for l in range(depth):
        are, aim, bmat, cmat, wsp, bsp, spw = _prep(
            ssm_lam_re[l], ssm_lam_im[l], ssm_log_dt[l], ssm_b_re[l], ssm_b_im[l],
            ssm_c_re[l], ssm_c_im[l], w_spatial[l], b_spatial[l])
        dsk = row(ssm_d[l])
        shared = dict(gmix=row(norm_mix_g[l]), w_in=w_in[l].astype(BF16), lng=row(ln_v_g[l]),
                      lnb=row(ln_v_b[l]))
        tail = dict(are=are, aim=aim, bmat=bmat, cmat=cmat, dsk=dsk,
                    w_a=w_branch_a[l].astype(BF16), w_b=w_branch_b[l].astype(BF16),
                    w_out=w_out[l].astype(BF16))
        ffn_w = (row(norm_ffn_g[l]), w_gate_ffn[l], w_up_ffn[l], w_down_ffn[l])

        xp, hre, him = _mixer_prompt(xp, **shared, wsp=wsp, bsp=bsp, **tail)
        p_re.append(hre.reshape(nb, SSM_GROUPS, SSM_STATE))
        p_im.append(him.reshape(nb, SSM_GROUPS, SSM_STATE))

        xs, hre, him, v_rows = _mixer_sample(
            xs, state_ssm_re[l].reshape(ns, STATE_W), state_ssm_im[l].reshape(ns, STATE_W),
            **shared, spw=spw, bsp=bsp, **tail)
        s_re.append(hre.reshape(ns, SSM_GROUPS, SSM_STATE))
        s_im.append(him.reshape(ns, SSM_GROUPS, SSM_STATE))
        s_v.append(v_rows)

        xp, xs = _ffn(xp.reshape(nb * seq, d), xs, *ffn_w, gfin, FFN_BLOCK_ROWS)
        xp = xp.reshape(nb, seq, d)

    return (xp, xs, jnp.stack(p_re), jnp.stack(p_im), jnp.stack(s_re),
            jnp.stack(s_im), jnp.stack(s_v))
```

```python
import jax
import jax.numpy as jnp
from jax import lax
from jax.experimental import pallas as pl
from jax.experimental.pallas import tpu as pltpu

F32 = jnp.float32
BF16 = jnp.bfloat16

D_MODEL = 1024
CHUNK = 128
GMLP_GROUPS = 8
GROUP_CH = D_MODEL // GMLP_GROUPS
SSM_WIDTH = 512
SSM_GROUPS = 32
SSM_STATE = 64
SSM_GROUP_CH = 16
LANES = 128
SUBLANES = 8
SLABS = SSM_WIDTH // LANES
SLAB_GROUPS = LANES // SSM_GROUP_CH
SLAB_STATE = SLAB_GROUPS * SSM_STATE
STATE_W = SSM_GROUPS * SSM_STATE
EPS = 1e-6

OFF_U, OFF_V, OFF_S = 0, D_MODEL, 2 * D_MODEL
OFF_GA = 2 * D_MODEL + SSM_WIDTH
OFF_GB = OFF_GA + D_MODEL

ROW_BLOCK = 256
SCAN_STEPS = 16
FFN_BLOCK_ROWS = 512
FFN_SUB_ROWS = 256
VMEM_LIMIT = 56 * 1024 * 1024


def _dot(a, b):
    return jnp.dot(a, b, preferred_element_type=F32)


def _rmsnorm(x, g):
    return x * lax.rsqrt(jnp.mean(x * x, axis=-1, keepdims=True) + EPS) * g


def _layernorm(x, g, b):
    mu = jnp.mean(x, axis=-1, keepdims=True)
    xc = x - mu
    var = jnp.mean(xc * xc, axis=-1, keepdims=True)
    return xc * lax.rsqrt(var + EPS) * g + b


def _const_spec(shape):
    zeros = (0,) * len(shape)
    return pl.BlockSpec(shape, lambda *_: zeros, pipeline_mode=pl.Buffered(1))


STAGE_SLOTS = 8
STAGE_ROWS, STAGE_COLS = 256, 1024


def _cast_pieces(src_hbm, dst_vmem):
    n_rows, n_cols = src_hbm.shape
    assert n_rows % STAGE_ROWS == 0 and n_cols % LANES == 0
    pieces = []
    for r in range(0, n_rows, STAGE_ROWS):
        for c in range(0, n_cols, STAGE_COLS):
            w = min(STAGE_COLS, n_cols - c)
            pieces.append((src_hbm.at[r:r + STAGE_ROWS, c:c + w],
                           dst_vmem.at[r:r + STAGE_ROWS, c:c + w], w))
    return pieces


def _load_cast(pieces, stage_view, sems):
    def copy(k):
        src, _, width = pieces[k]
        slot = k % STAGE_SLOTS
        return pltpu.make_async_copy(src, stage_view(slot, width), sems.at[slot])

    for k in range(min(STAGE_SLOTS, len(pieces))):
        copy(k).start()
    for k, (_, dst, width) in enumerate(pieces):
        copy(k).wait()
        dst[...] = stage_view(k % STAGE_SLOTS, width)[...].astype(BF16)
        if k + STAGE_SLOTS < len(pieces):
            copy(k + STAGE_SLOTS).start()


def _dot_exact(a, b):
    return jnp.dot(a, b, preferred_element_type=F32, precision=lax.Precision.HIGHEST)


def _group_mask(shape, row_div, lane_div):
    rows = lax.broadcasted_iota(jnp.int32, shape, 0) // row_div
    lanes = lax.broadcasted_iota(jnp.int32, shape, 1) // lane_div
    return rows == lanes


def _prep_kernel(lre, lim, ldt, bre, bim, cre_in, cim_in, wsp_in, bsp_in,
                 are_o, aim_o, bmat_o, cmat_o, wsp_o, bsp_o, spw_o):
    def slab_rows(row_of_group):
        slabs = [[row_of_group(j * SLAB_GROUPS + k) for k in range(SLAB_GROUPS)]
                 for j in range(SLABS)]
        return jnp.concatenate([jnp.concatenate(rows, axis=1) for rows in slabs], axis=0)

    lr = slab_rows(lambda g: lre[g:g + 1, :])
    li = slab_rows(lambda g: lim[g:g + 1, :])
    dt = jnp.exp(slab_rows(lambda g: jnp.broadcast_to(ldt[0:1, g:g + 1], (1, SSM_STATE))))
    mag = jnp.exp(lr * dt)
    a_re, a_im = mag * jnp.cos(li * dt), mag * jnp.sin(li * dt)
    n_re, n_im = a_re - 1.0, a_im
    den = lr * lr + li * li
    k_re = (n_re * lr + n_im * li) / den
    k_im = (n_im * lr - n_re * li) / den

    rep_h = (lax.broadcasted_iota(jnp.int32, (SSM_GROUP_CH, LANES), 1) % SSM_GROUP_CH
             == lax.broadcasted_iota(jnp.int32, (SSM_GROUP_CH, LANES), 0)).astype(F32)
    rep_p = (lax.broadcasted_iota(jnp.int32, (SSM_STATE, SLAB_STATE), 1) % SSM_STATE
             == lax.broadcasted_iota(jnp.int32, (SSM_STATE, SLAB_STATE), 0)).astype(F32)
    tb_re = _dot_exact(bre[...], rep_h)
    tb_im = _dot_exact(bim[...], rep_h)
    tc_re = _dot_exact(cre_in[...], rep_p)
    tc_im = _dot_exact(cim_in[...], rep_p)
    mask_b = _group_mask((SLAB_STATE, LANES), SSM_STATE, SSM_GROUP_CH)
    mask_c = _group_mask((LANES, SLAB_STATE), SSM_GROUP_CH, SSM_STATE)

    for j in range(SLABS):
        lanes = slice(j * SLAB_STATE, (j + 1) * SLAB_STATE)
        are_o[:, lanes] = jnp.broadcast_to(a_re[j:j + 1, :], (SUBLANES, SLAB_STATE))
        aim_o[:, lanes] = jnp.broadcast_to(a_im[j:j + 1, :], (SUBLANES, SLAB_STATE))
        rows = slice(j * SLAB_STATE, (j + 1) * SLAB_STATE)
        m_re = jnp.where(mask_b, tb_re[rows], 0.0).T
        m_im = jnp.where(mask_b, tb_im[rows], 0.0).T
        c_re, c_im = k_re[j:j + 1, :], k_im[j:j + 1, :]
        bmat_o[j] = jnp.concatenate([c_re * m_re - c_im * m_im, c_re * m_im + c_im * m_re],
                                    axis=1).astype(BF16)
        rows = slice(j * LANES, (j + 1) * LANES)
        read = jnp.concatenate([jnp.where(mask_c, tc_re[rows], 0.0),
                                jnp.where(mask_c, -tc_im[rows], 0.0)], axis=1)
        cmat_o[j] = read.T.astype(BF16)

    tril = (lax.broadcasted_iota(jnp.int32, (CHUNK, CHUNK), 0)
            >= lax.broadcasted_iota(jnp.int32, (CHUNK, CHUNK), 1))
    spw = []
    for g in range(GMLP_GROUPS):
        w = wsp_in[g]
        wsp_o[g] = jnp.where(tril, w, 0.0).astype(BF16)
        spw.append(jnp.broadcast_to(w[0:1, 0:1], (1, GROUP_CH)))
    spw_o[...] = jnp.concatenate(spw, axis=1)
    sel = _group_mask((D_MODEL, GMLP_GROUPS), GROUP_CH, 1).astype(F32)
    bsp_o[...] = _dot_exact(sel, bsp_in[...]).T


def _prep(lam_re, lam_im, log_dt, b_re, b_im, c_re, c_im, w_spatial, b_spatial):
    g, p, h = b_re.shape
    return pl.pallas_call(
        _prep_kernel,
        out_shape=(jax.ShapeDtypeStruct((SUBLANES, STATE_W), F32),
                   jax.ShapeDtypeStruct((SUBLANES, STATE_W), F32),
                   jax.ShapeDtypeStruct((SLABS, LANES, 2 * SLAB_STATE), BF16),
                   jax.ShapeDtypeStruct((SLABS, 2 * SLAB_STATE, LANES), BF16),
                   jax.ShapeDtypeStruct(w_spatial.shape, BF16),
                   jax.ShapeDtypeStruct((CHUNK, D_MODEL), F32),
                   jax.ShapeDtypeStruct((1, D_MODEL), F32)),
        name="prep",
    )(lam_re, lam_im, log_dt.reshape(1, g), b_re.reshape(g * p, h), b_im.reshape(g * p, h),
      c_re.reshape(g * h, p), c_im.reshape(g * h, p), w_spatial, b_spatial)


def _mixer_prompt_kernel(x_ref, gmix, w_in, lng, lnb, wsp, bsp, are, aim, bmat, cmat, dsk,
                         w_a, w_b, w_out,
                         x1_ref, hre_o, him_o,
                         s_tm, ys_tm, ya_s, hn_s, bu, hbuf, st_re, st_im):
    i = pl.program_id(0)
    nb = x_ref.shape[0]
    rows = nb * CHUNK
    per_block = ROW_BLOCK // CHUNK

    @pl.when(i == 0)
    def _():
        st_re[...] = jnp.zeros_like(st_re)
        st_im[...] = jnp.zeros_like(st_im)

    def project(blk):
        r = slice(blk * ROW_BLOCK, (blk + 1) * ROW_BLOCK)
        x = jnp.concatenate([x_ref[blk * per_block + k] for k in range(per_block)], axis=0)
        hn = _rmsnorm(x, gmix[...]).astype(BF16)
        hn_s[r, :] = hn
        zu = _dot(hn, w_in[:, OFF_U:OFF_U + D_MODEL])
        zv = _dot(hn, w_in[:, OFF_V:OFF_V + D_MODEL])
        s = _dot(hn, w_in[:, OFF_S:OFF_S + SSM_WIDTH])
        for k in range(per_block):
            b = blk * per_block + k
            for j in range(SLABS):
                s_tm[j, pl.ds(b, CHUNK, stride=nb), :] = s[k * CHUNK:(k + 1) * CHUNK,
                                                          j * LANES:(j + 1) * LANES]
        return zu, zv

    def gate(blk, zu, zv):
        u = jax.nn.gelu(zu)
        vb = _layernorm(jax.nn.gelu(zv), lng[...], lnb[...]).astype(BF16)
        wide = [_dot(wsp[g], jnp.concatenate(
                    [vb[k * CHUNK:(k + 1) * CHUNK, g * GROUP_CH:(g + 1) * GROUP_CH]
                     for k in range(per_block)], axis=1))
                for g in range(GMLP_GROUPS)]
        mixed = [jnp.concatenate([w[:, k * GROUP_CH:(k + 1) * GROUP_CH] for w in wide], axis=1)
                 + bsp[...] for k in range(per_block)]
        gated = (u * jnp.concatenate(mixed, axis=0)).astype(BF16)
        ya_s[blk * ROW_BLOCK:(blk + 1) * ROW_BLOCK, :] = _dot(gated, w_a[...])

    n_blk = rows // ROW_BLOCK
    z = project(0)
    for blk in range(n_blk):
        z_next = project(blk + 1) if blk + 1 < n_blk else None
        gate(blk, *z)
        z = z_next

    sub_rows = SCAN_STEPS * nb

    def project_in(q):
        for j in range(SLABS):
            s_blk = s_tm[j, q * sub_rows:(q + 1) * sub_rows, :]
            bu[q % 2, j] = _dot(s_blk.astype(BF16), bmat[j])

    def recur(q):
        for j in range(SLABS):
            lanes = slice(j * SLAB_STATE, (j + 1) * SLAB_STATE)
            a_re, a_im = are[:, lanes], aim[:, lanes]
            h_re, h_im = st_re[:, lanes], st_im[:, lanes]
            for t in range(0, SCAN_STEPS, 2):
                pair = []
                for r in (slice(t * nb, (t + 1) * nb), slice((t + 1) * nb, (t + 2) * nb)):
                    n_re = a_re * h_re - a_im * h_im + bu[q % 2, j, r, 0:SLAB_STATE]
                    n_im = a_re * h_im + a_im * h_re + bu[q % 2, j, r, SLAB_STATE:2 * SLAB_STATE]
                    h_re, h_im = n_re, n_im
                    pair.append(jnp.concatenate([n_re, n_im], axis=1))
                hbuf[q % 2, j, t * nb:(t + 2) * nb, :] = jnp.concatenate(pair, axis=0).astype(BF16)
            st_re[:, lanes] = h_re
            st_im[:, lanes] = h_im

    def project_out(q):
        for j in range(SLABS):
            r = slice(q * sub_rows, (q + 1) * sub_rows)
            d_j = dsk[:, j * LANES:(j + 1) * LANES]
            ys_tm[j, r, :] = _dot(hbuf[q % 2, j], cmat[j]) + d_j * s_tm[j, r, :]

    n_sub = rows // sub_rows
    project_in(0)
    for q in range(n_sub):
        if q + 1 < n_sub:
            project_in(q + 1)
        recur(q)
        if q >= 1:
            project_out(q - 1)
    project_out(n_sub - 1)

    def merge(blk):
        r = slice(blk * ROW_BLOCK, (blk + 1) * ROW_BLOCK)
        ys = jnp.concatenate(
            [jnp.concatenate([ys_tm[j, pl.ds(blk * per_block + k, CHUNK, stride=nb), :]
                              for j in range(SLABS)], axis=1)
             for k in range(per_block)], axis=0)
        pb = _dot(ys.astype(BF16), w_b[...])
        yb = pb[:, :D_MODEL] * jax.nn.sigmoid(pb[:, D_MODEL:])
        hn = hn_s[r, :]
        ga = jax.nn.sigmoid(_dot(hn, w_in[:, OFF_GA:OFF_GA + D_MODEL]))
        gb = jax.nn.sigmoid(_dot(hn, w_in[:, OFF_GB:OFF_GB + D_MODEL]))
        merged = (ga * ya_s[r, :] + gb * yb).astype(BF16)
        y = _dot(merged, w_out[...])
        for k in range(per_block):
            b = blk * per_block + k
            x1_ref[b] = y[k * CHUNK:(k + 1) * CHUNK]

    for blk in range(n_blk):
        merge(blk)

    @pl.when(i == pl.num_programs(0) - 1)
    def _():
        hre_o[...] = st_re[...]
        him_o[...] = st_im[...]


def _mixer_prompt(x, gmix, w_in, lng, lnb, wsp, bsp, are, aim, bmat, cmat, dsk, w_a, w_b, w_out):
    nb, seq, d = x.shape
    rows = nb * CHUNK
    consts = (gmix, w_in, lng, lnb, wsp, bsp, are, aim, bmat, cmat, dsk, w_a, w_b, w_out)
    x_spec = pl.BlockSpec((nb, CHUNK, d), lambda i: (0, i, 0))
    st_spec = pl.BlockSpec((nb, STATE_W), lambda i: (0, 0))
    return pl.pallas_call(
        _mixer_prompt_kernel,
        grid=(seq // CHUNK,),
        in_specs=[x_spec] + [_const_spec(c.shape) for c in consts],
        out_specs=(x_spec, st_spec, st_spec),
        out_shape=(jax.ShapeDtypeStruct(x.shape, F32),
                   jax.ShapeDtypeStruct((nb, STATE_W), F32),
                   jax.ShapeDtypeStruct((nb, STATE_W), F32)),
        scratch_shapes=[
            pltpu.VMEM((SLABS, rows, LANES), F32),
            pltpu.VMEM((SLABS, rows, LANES), F32),
            pltpu.VMEM((rows, d), F32),
            pltpu.VMEM((rows, d), BF16),
            pltpu.VMEM((2, SLABS, SCAN_STEPS * nb, 2 * SLAB_STATE), F32),
            pltpu.VMEM((2, SLABS, SCAN_STEPS * nb, 2 * SLAB_STATE), BF16),
            pltpu.VMEM((nb, STATE_W), F32),
            pltpu.VMEM((nb, STATE_W), F32),
        ],
        compiler_params=pltpu.CompilerParams(dimension_semantics=("arbitrary",),
                                             vmem_limit_bytes=VMEM_LIMIT),
        name="mixer_prompt",
    )(x, *consts)


def _mixer_sample_kernel(x_ref, h0re, h0im, gmix, w_in, lng, lnb, spw, bsp, are, aim, bmat, cmat,
                         dsk, w_a, w_b, w_out,
                         x1_ref, hre_o, him_o, v_o):
    x = x_ref[:, 0, :]
    hn = _rmsnorm(x, gmix[...]).astype(BF16)
    u = jax.nn.gelu(_dot(hn, w_in[:, OFF_U:OFF_U + D_MODEL]))
    v = _layernorm(jax.nn.gelu(_dot(hn, w_in[:, OFF_V:OFF_V + D_MODEL])), lng[...], lnb[...])
    v_o[:, 0, :] = v
    mixed = v * spw[...] + bsp[0:1, :]
    ya = _dot((u * mixed).astype(BF16), w_a[...])
    s = _dot(hn, w_in[:, OFF_S:OFF_S + SSM_WIDTH])
    ys = []
    for j in range(SLABS):
        lanes = slice(j * SLAB_STATE, (j + 1) * SLAB_STATE)
        s_j = s[:, j * LANES:(j + 1) * LANES]
        bu = _dot(s_j.astype(BF16), bmat[j])
        a_re, a_im = are[0:1, lanes], aim[0:1, lanes]
        p_re, p_im = h0re[:, lanes], h0im[:, lanes]
        n_re = a_re * p_re - a_im * p_im + bu[:, 0:SLAB_STATE]
        n_im = a_re * p_im + a_im * p_re + bu[:, SLAB_STATE:]
        hre_o[:, lanes] = n_re
        him_o[:, lanes] = n_im
        h = jnp.concatenate([n_re, n_im], axis=1).astype(BF16)
        ys.append(_dot(h, cmat[j]) + dsk[:, j * LANES:(j + 1) * LANES] * s_j)
    pb = _dot(jnp.concatenate(ys, axis=1).astype(BF16), w_b[...])
    yb = pb[:, :D_MODEL] * jax.nn.sigmoid(pb[:, D_MODEL:])
    ga = jax.nn.sigmoid(_dot(hn, w_in[:, OFF_GA:OFF_GA + D_MODEL]))
    gb = jax.nn.sigmoid(_dot(hn, w_in[:, OFF_GB:OFF_GB + D_MODEL]))
    merged = (ga * ya + gb * yb).astype(BF16)
    x1_ref[...] = x + _dot(merged, w_out[...])


def _mixer_sample(x, h0re, h0im, gmix, w_in, lng, lnb, spw, bsp, are, aim, bmat, cmat, dsk,
                  w_a, w_b, w_out):
    n, _, d = x.shape
    return pl.pallas_call(
        _mixer_sample_kernel,
        out_shape=(jax.ShapeDtypeStruct((n, d), F32),
                   jax.ShapeDtypeStruct((n, STATE_W), F32),
                   jax.ShapeDtypeStruct((n, STATE_W), F32),
                   jax.ShapeDtypeStruct((n, 1, d), F32)),
        compiler_params=pltpu.CompilerParams(vmem_limit_bytes=VMEM_LIMIT),
        name="mixer_sample",
    )(x, h0re, h0im, gmix, w_in, lng, lnb, spw, bsp, are, aim, bmat, cmat, dsk, w_a, w_b, w_out)


def _ffn_rows(x, gffn, w_gate, w_up, w_down, gfin):
    h = _rmsnorm(x, gffn[...]).astype(BF16)
    act = (jax.nn.silu(_dot(h, w_gate[...])) * _dot(h, w_up[...])).astype(BF16)
    x2 = x + _dot(act, w_down[...])
    return _rmsnorm(x2, gfin[...])


def _ffn_kernel(x_ref, res_ref, xs_ref, gffn, wg_hbm, wu_hbm, wd_hbm, gfin, y_ref, ys_ref,
                w_gate, w_up, w_down, stage, sems):
    @pl.when(pl.program_id(0) == 0)
    def _():
        pieces = (_cast_pieces(wg_hbm, w_gate) + _cast_pieces(wu_hbm, w_up)
                  + _cast_pieces(wd_hbm, w_down))
        _load_cast(pieces, lambda slot, width: stage.at[slot, :, 0:width], sems)

    params = (gffn, w_gate, w_up, w_down, gfin)
    for r0 in range(0, x_ref.shape[0], FFN_SUB_ROWS):
        r = slice(r0, r0 + FFN_SUB_ROWS)
        y_ref[r, :] = _ffn_rows(res_ref[r, :] + x_ref[r, :], *params)

    @pl.when(pl.program_id(0) == pl.num_programs(0) - 1)
    def _():
        ys_ref[:, 0, :] = _ffn_rows(xs_ref[...], *params)


def _ffn(x, res, xs, gffn, w_gate, w_up, w_down, gfin, block_rows):
    n, d = x.shape
    ns = xs.shape[0]
    row_spec = pl.BlockSpec((block_rows, d), lambda i: (i, 0))
    hbm_spec = pl.BlockSpec(memory_space=pl.ANY)
    return pl.pallas_call(
        _ffn_kernel,
        grid=(n // block_rows,),
        in_specs=[row_spec, row_spec, _const_spec(xs.shape), _const_spec(gffn.shape), hbm_spec, hbm_spec,
                  hbm_spec, _const_spec(gfin.shape)],
        out_specs=(row_spec, pl.BlockSpec((ns, 1, d), lambda i: (0, 0, 0))),
        out_shape=(jax.ShapeDtypeStruct((n, d), F32), jax.ShapeDtypeStruct((ns, 1, d), F32)),
        scratch_shapes=[pltpu.VMEM(w_gate.shape, BF16), pltpu.VMEM(w_up.shape, BF16),
                        pltpu.VMEM(w_down.shape, BF16),
                        pltpu.VMEM((STAGE_SLOTS, STAGE_ROWS, STAGE_COLS), F32),
                        pltpu.SemaphoreType.DMA((STAGE_SLOTS,))],
        compiler_params=pltpu.CompilerParams(dimension_semantics=("arbitrary",),
                                             vmem_limit_bytes=VMEM_LIMIT),
        name="ffn",
    )(x, res, xs, gffn, w_gate, w_up, w_down, gfin)


def kernel(x_prompt, x_sample, state_ssm_re, state_ssm_im, norm_mix_g, w_in, ln_v_g, ln_v_b,
           w_spatial, b_spatial, ssm_lam_re, ssm_lam_im, ssm_log_dt, ssm_b_re, ssm_b_im,
           ssm_c_re, ssm_c_im, ssm_d, w_branch_a, w_branch_b, w_out, norm_ffn_g,
           w_gate_ffn, w_up_ffn, w_down_ffn, norm_final_g):
    depth = w_in.shape[0]
    assert depth == 1
    nb, seq, d = x_prompt.shape
    ns = x_sample.shape[0]
    row = lambda a: a.reshape(1, -1)

    xp = x_prompt
    xs = x_sample
    gfin = row(norm_final_g)
    p_re, p_im, s_re, s_im, s_v = [], [], [], [], []
    for l in range(depth):
        are, aim, bmat, cmat, wsp, bsp, spw = _prep(
            ssm_lam_re[l], ssm_lam_im[l], ssm_log_dt[l], ssm_b_re[l], ssm_b_im[l],
            ssm_c_re[l], ssm_c_im[l], w_spatial[l], b_spatial[l])
        dsk = row(ssm_d[l])
        shared = dict(gmix=row(norm_mix_g[l]), w_in=w_in[l].astype(BF16), lng=row(ln_v_g[l]),
                      lnb=row(ln_v_b[l]))
        tail = dict(are=are, aim=aim, bmat=bmat, cmat=cmat, dsk=dsk,
                    w_a=w_branch_a[l].astype(BF16), w_b=w_branch_b[l].astype(BF16),
                    w_out=w_out[l].astype(BF16))
        ffn_w = (row(norm_ffn_g[l]), w_gate_ffn[l], w_up_ffn[l], w_down_ffn[l])

        xp, hre, him = _mixer_prompt(xp, **shared, wsp=wsp, bsp=bsp, **tail)
        p_re.append(hre.reshape(nb, SSM_GROUPS, SSM_STATE))
        p_im.append(him.reshape(nb, SSM_GROUPS, SSM_STATE))

        xs, hre, him, v_rows = _mixer_sample(
            xs, state_ssm_re[l].reshape(ns, STATE_W), state_ssm_im[l].reshape(ns, STATE_W),
            **shared, spw=spw, bsp=bsp, **tail)
        s_re.append(hre.reshape(ns, SSM_GROUPS, SSM_STATE))
        s_im.append(him.reshape(ns, SSM_GROUPS, SSM_STATE))
        s_v.append(v_rows)

        xp, xs = _ffn(xp.reshape(nb * seq, d), x_prompt.reshape(nb * seq, d), xs, *ffn_w, gfin,
                      FFN_BLOCK_ROWS)
        xp = xp.reshape(nb, seq, d)

    return (xp, xs, jnp.stack(p_re), jnp.stack(p_im), jnp.stack(s_re),
            jnp.stack(s_im), jnp.stack(s_v))
```
